```python
import jax, jax.numpy as jnp
from jax import lax
import numpy as np

D_MODEL = 1024
BATCH = 2
SEQ = 8192
DEPTH = 2

MEM_LEN = 256
NSA_HEADS = 8
NSA_KV_HEADS = 2
NSA_HEAD_DIM = 64
NSA_REP = NSA_HEADS // NSA_KV_HEADS
CMP_BLOCK = 32
CMP_STRIDE = 16
SEL_BLOCK = 64
N_SELECT = 16
WINDOW = 512
Q_BLOCK = 128
NSA_WIDTH = NSA_HEADS * NSA_HEAD_DIM
KV_WIDTH = NSA_KV_HEADS * NSA_HEAD_DIM
GMLP_WIDTH = 256
GMLP_GROUPS = 4
GMLP_CHUNK = 128
CONV_WIDTH = 256
CONV_TAPS = 31
MIX_WIDTH = NSA_WIDTH + GMLP_WIDTH + CONV_WIDTH
IN_SPLITS = (NSA_WIDTH, KV_WIDTH, KV_WIDTH, KV_WIDTH, KV_WIDTH, KV_WIDTH, KV_WIDTH,
             NSA_HEADS * 3, 2 * GMLP_WIDTH, 2 * CONV_WIDTH)
IN_WIDTH = sum(IN_SPLITS)
XATTN_HEADS = 4
XATTN_HEAD_DIM = D_MODEL // XATTN_HEADS
D_FF = 2816
FFN_CONV_TAPS = 3

EPS = 1e-6
NEG_INF = -1e30
TINY = 1e-30
FORCE = 1e4

kernel_name = "hymba_style_nsa_gmlp_conformer_hybrid"


def _split_cols(z, widths):
    outs, off = [], 0
    for w in widths:
        outs.append(z[..., off:off + w])
        off += w
    return outs


def rms_norm(x, g):
    xf = x.astype(jnp.float32)
    y = xf * lax.rsqrt(jnp.mean(xf * xf, axis=-1, keepdims=True) + EPS)
    return (y * g.astype(jnp.float32)).astype(x.dtype)


def layer_norm(x, g, b):
    xf = x.astype(jnp.float32)
    mu = jnp.mean(xf, axis=-1, keepdims=True)
    var = jnp.mean(jnp.square(xf - mu), axis=-1, keepdims=True)
    y = (xf - mu) * lax.rsqrt(var + EPS)
    return (y * g.astype(jnp.float32) + b.astype(jnp.float32)).astype(x.dtype)


def masked_softmax(s, mask):
    s = jnp.where(mask, s, NEG_INF)
    m = jnp.max(s, axis=-1, keepdims=True)
    e = jnp.where(mask, jnp.exp(s - m), 0.0)
    return e / jnp.maximum(jnp.sum(e, axis=-1, keepdims=True), TINY)


def causal_depthwise_conv(x, w, b):
    taps = w.shape[0]
    y = lax.conv_general_dilated(x, w[:, None, :], window_strides=(1,), padding=[(taps - 1, 0)],
                                 dimension_numbers=("NWC", "WIO", "NWC"),
                                 feature_group_count=x.shape[-1])
    return y + b


def _to_heads(z, n):
    b, s, _ = z.shape
    return z.reshape(b, s, n, NSA_HEAD_DIM).transpose(0, 2, 1, 3)


def compress_blocks(kv, pe, w1, w2):
    b, g, s, dh = kv.shape
    n_cmp = (s - CMP_BLOCK) // CMP_STRIDE + 1
    idx = jnp.arange(n_cmp)[:, None] * CMP_STRIDE + jnp.arange(CMP_BLOCK)[None, :]
    blocks = kv[:, :, idx, :] + pe
    flat = blocks.reshape(b, g, n_cmp, CMP_BLOCK * dh)
    return jax.nn.gelu(flat @ w1) @ w2


def nsa_mixer(q, k_c_raw, v_c_raw, k_s, v_s, k_w, v_w, gate_logits, cmp_pe, cmp_w1, cmp_w2):
    b, s, _ = q.shape
    dh, G, R = NSA_HEAD_DIM, NSA_KV_HEADS, NSA_REP
    scale = dh ** -0.5
    k_cmp = compress_blocks(_to_heads(k_c_raw, G), cmp_pe[0], cmp_w1[0], cmp_w2[0])
    v_cmp = compress_blocks(_to_heads(v_c_raw, G), cmp_pe[1], cmp_w1[1], cmp_w2[1])
    n_cmp = k_cmp.shape[2]
    cmp_end = jnp.arange(n_cmp) * CMP_STRIDE + CMP_BLOCK - 1
    n_slc = s // SEL_BLOCK
    k_sel = min(N_SELECT, n_slc)
    k_slc_b = _to_heads(k_s, G).reshape(b, G, n_slc, SEL_BLOCK, dh)
    v_slc_b = _to_heads(v_s, G).reshape(b, G, n_slc, SEL_BLOCK, dh)
    ci = jnp.arange(n_cmp)[:, None] * CMP_STRIDE
    sj = jnp.arange(n_slc)[None, :] * SEL_BLOCK
    overlap = ((ci < sj + SEL_BLOCK) & (ci + CMP_BLOCK > sj)).astype(jnp.float32)
    pad = ((0, 0), (0, 0), (WINDOW, 0), (0, 0))
    k_win_p = jnp.pad(_to_heads(k_w, G), pad)
    v_win_p = jnp.pad(_to_heads(v_w, G), pad)
    nb = s // Q_BLOCK
    q_blocks = q.reshape(b, nb, Q_BLOCK, G, R, dh).transpose(1, 0, 3, 4, 2, 5)
    gather = jax.vmap(jax.vmap(lambda kb, ix: kb[ix]))

    def block_fn(args):
        qb, bi = args
        t = bi * Q_BLOCK + jnp.arange(Q_BLOCK)
        s_c = jnp.einsum("bgrtd,bgnd->bgrtn", qb, k_cmp).astype(jnp.float32) * scale
        p_c = masked_softmax(s_c, cmp_end[None, :] <= t[:, None])
        o_c = jnp.einsum("bgrtn,bgnd->bgrtd", p_c.astype(v_cmp.dtype), v_cmp)
        imp = jnp.einsum("bgtn,nj->bgtj", jnp.sum(p_c, axis=2), overlap)
        cur = (t // SEL_BLOCK)[:, None]
        jj = jnp.arange(n_slc)[None, :]
        forced = (jj == 0) | (jj == cur) | (jj == cur - 1)
        imp = jnp.where(forced, FORCE, imp)
        imp = jnp.where(jj <= cur, imp, -FORCE)
        _, sel = lax.top_k(imp, k_sel)
        gk = gather(k_slc_b, sel).reshape(b, G, Q_BLOCK, k_sel * SEL_BLOCK, dh)
        gv = gather(v_slc_b, sel).reshape(b, G, Q_BLOCK, k_sel * SEL_BLOCK, dh)
        kpos = (sel[..., None] * SEL_BLOCK + jnp.arange(SEL_BLOCK)).reshape(b, G, Q_BLOCK, k_sel * SEL_BLOCK)
        mask_s = (kpos <= t[None, None, :, None])[:, :, None]
        s_s = jnp.einsum("bgrtd,bgtmd->bgrtm", qb, gk).astype(jnp.float32) * scale
        p_s = masked_softmax(s_s, mask_s)
        o_s = jnp.einsum("bgrtm,bgtmd->bgrtd", p_s.astype(gv.dtype), gv)
        kw = lax.dynamic_slice_in_dim(k_win_p, bi * Q_BLOCK, Q_BLOCK + WINDOW, axis=2)
        vw = lax.dynamic_slice_in_dim(v_win_p, bi * Q_BLOCK, Q_BLOCK + WINDOW, axis=2)
        kp = (bi * Q_BLOCK - WINDOW + jnp.arange(Q_BLOCK + WINDOW))[None, :]
        mask_w = (kp <= t[:, None]) & (kp > t[:, None] - WINDOW) & (kp >= 0)
        s_w = jnp.einsum("bgrtd,bgkd->bgrtk", qb, kw).astype(jnp.float32) * scale
        p_w = masked_softmax(s_w, mask_w)
        o_w = jnp.einsum("bgrtk,bgkd->bgrtd", p_w.astype(vw.dtype), vw)
        return o_c, o_s, o_w

    o_c, o_s, o_w = lax.map(block_fn, (q_blocks, jnp.arange(nb)))
    def unblock(o):
        return o.transpose(1, 0, 4, 2, 3, 5).reshape(b, s, NSA_HEADS, dh)
    gates = jax.nn.sigmoid(gate_logits.reshape(b, s, NSA_HEADS, 3))
    o = (gates[..., 0:1] * unblock(o_c) + gates[..., 1:2] * unblock(o_s)
         + gates[..., 2:3] * unblock(o_w))
    return o.reshape(b, s, NSA_WIDTH)


def gmlp_mixer(z, ln_g, ln_b, w_s, b_s):
    b, s, _ = z.shape
    z = jax.nn.gelu(z)
    u, v = z[..., :GMLP_WIDTH], z[..., GMLP_WIDTH:]
    v = layer_norm(v, ln_g, ln_b)
    nc = s // GMLP_CHUNK
    v = v.reshape(b, nc, GMLP_CHUNK, GMLP_GROUPS, GMLP_WIDTH // GMLP_GROUPS)
    causal = jnp.tril(jnp.ones((GMLP_CHUNK, GMLP_CHUNK), dtype=w_s.dtype))
    mixed = jnp.einsum("gts,bcsgd->bctgd", w_s * causal, v) + b_s.T[:, :, None]
    return u * mixed.reshape(b, s, GMLP_WIDTH)


def conformer_conv_mixer(z, dw_w, dw_b, ln_g, ln_b):
    a, gt = z[..., :CONV_WIDTH], z[..., CONV_WIDTH:]
    h = a * jax.nn.sigmoid(gt)
    h = causal_depthwise_conv(h, dw_w, dw_b)
    h = layer_norm(h, ln_g, ln_b)
    return jax.nn.silu(h)


def memory_cross_attention(x, mem, g_x, g_m, wq, wk, wv, wo):
    b, s, _ = x.shape
    h = rms_norm(x, g_x)
    m = rms_norm(mem, g_m)
    q = (h @ wq).reshape(b, s, XATTN_HEADS, XATTN_HEAD_DIM)
    k = (m @ wk).reshape(b, m.shape[1], XATTN_HEADS, XATTN_HEAD_DIM)
    v = (m @ wv).reshape(b, m.shape[1], XATTN_HEADS, XATTN_HEAD_DIM)
    sc = jnp.einsum("bshd,bmhd->bhsm", q, k).astype(jnp.float32) * (XATTN_HEAD_DIM ** -0.5)
    p = jax.nn.softmax(sc, axis=-1)
    o = jnp.einsum("bhsm,bmhd->bshd", p.astype(v.dtype), v).reshape(b, s, D_MODEL)
    return o @ wo


def conv_ffn(x, g, w_up, dw_w, dw_b, w_down):
    h = causal_depthwise_conv(rms_norm(x, g) @ w_up, dw_w, dw_b)
    gate, up = h[..., :D_FF], h[..., D_FF:]
    return (jax.nn.silu(gate) * up) @ w_down


def setup_inputs(seed: int = 0) -> dict:
    key = jax.random.key(seed)
    ks = iter(jax.random.split(key, 40))
    def nrm(shape, scale):
        return jax.random.normal(next(ks), shape, jnp.float32) * scale
    def gain(shape):
        return 1.0 + 0.02 * jax.random.normal(next(ks), shape, jnp.float32)
    L, dh = DEPTH, NSA_HEAD_DIM
    return {
        "x": nrm((BATCH, SEQ, D_MODEL), 1.0),
        "mem": nrm((BATCH, MEM_LEN, D_MODEL), 1.0),
        "mix_norm_g": gain((L, D_MODEL)),
        "w_in": nrm((L, D_MODEL, IN_WIDTH), D_MODEL ** -0.5),
        "cmp_pe": nrm((L, 2, CMP_BLOCK, dh), 0.1),
        "cmp_w1": nrm((L, 2, CMP_BLOCK * dh, dh), (CMP_BLOCK * dh) ** -0.5),
        "cmp_w2": nrm((L, 2, dh, dh), dh ** -0.5),
        "gmlp_ln_g": gain((L, GMLP_WIDTH)),
        "gmlp_ln_b": nrm((L, GMLP_WIDTH), 0.02),
        "gmlp_ws": nrm((L, GMLP_GROUPS, GMLP_CHUNK, GMLP_CHUNK), GMLP_CHUNK ** -0.5),
        "gmlp_bs": gain((L, GMLP_GROUPS, GMLP_CHUNK)),
        "conv_dw_w": nrm((L, CONV_TAPS, CONV_WIDTH), CONV_TAPS ** -0.5),
        "conv_dw_b": nrm((L, CONV_WIDTH), 0.02),
        "conv_ln_g": gain((L, CONV_WIDTH)),
        "conv_ln_b": nrm((L, CONV_WIDTH), 0.02),
        "mix_out_g": gain((L, MIX_WIDTH)),
        "w_out": nrm((L, MIX_WIDTH, D_MODEL), MIX_WIDTH ** -0.5),
        "xattn_norm_g": gain((L, D_MODEL)),
        "mem_norm_g": gain((L, D_MODEL)),
        "xattn_wq": nrm((L, D_MODEL, D_MODEL), D_MODEL ** -0.5),
        "xattn_wk": nrm((L, D_MODEL, D_MODEL), D_MODEL ** -0.5),
        "xattn_wv": nrm((L, D_MODEL, D_MODEL), D_MODEL ** -0.5),
        "xattn_wo": nrm((L, D_MODEL, D_MODEL), D_MODEL ** -0.5),
        "ffn_norm_g": gain((L, D_MODEL)),
        "ffn_w_up": nrm((L, D_MODEL, 2 * D_FF), D_MODEL ** -0.5),
        "ffn_dw_w": nrm((L, FFN_CONV_TAPS, 2 * D_FF), FFN_CONV_TAPS ** -0.5),
        "ffn_dw_b": nrm((L, 2 * D_FF), 0.02),
        "ffn_w_down": nrm((L, D_FF, D_MODEL), D_FF ** -0.5),
        "final_norm_g": gain((D_MODEL,)),
    }


def reference(x, mem, mix_norm_g, w_in, cmp_pe, cmp_w1, cmp_w2, gmlp_ln_g, gmlp_ln_b, gmlp_ws,
              gmlp_bs, conv_dw_w, conv_dw_b, conv_ln_g, conv_ln_b, mix_out_g, w_out,
              xattn_norm_g, mem_norm_g, xattn_wq, xattn_wk, xattn_wv, xattn_wo,
              ffn_norm_g, ffn_w_up, ffn_dw_w, ffn_dw_b, ffn_w_down, final_norm_g):
    for l in range(DEPTH):
        z = rms_norm(x, mix_norm_g[l]) @ w_in[l]
        (q, kc, vc, ks, vs, kw, vw, gl, zg, zc) = _split_cols(z, IN_SPLITS)
        y_nsa = nsa_mixer(q, kc, vc, ks, vs, kw, vw, gl, cmp_pe[l], cmp_w1[l], cmp_w2[l])
        y_gmlp = gmlp_mixer(zg, gmlp_ln_g[l], gmlp_ln_b[l], gmlp_ws[l], gmlp_bs[l])
        y_conv = conformer_conv_mixer(zc, conv_dw_w[l], conv_dw_b[l], conv_ln_g[l], conv_ln_b[l])
        g_nsa, g_gmlp, g_conv = _split_cols(mix_out_g[l], (NSA_WIDTH, GMLP_WIDTH, CONV_WIDTH))
        y = jnp.concatenate([rms_norm(y_nsa, g_nsa), rms_norm(y_gmlp, g_gmlp),
                             rms_norm(y_conv, g_conv)], axis=-1)
        x = x + y @ w_out[l]
        x = x + memory_cross_attention(x, mem, xattn_norm_g[l], mem_norm_g[l], xattn_wq[l],
                                       xattn_wk[l], xattn_wv[l], xattn_wo[l])
        x = x + conv_ffn(x, ffn_norm_g[l], ffn_w_up[l], ffn_dw_w[l], ffn_dw_b[l], ffn_w_down[l])
    return rms_norm(x, final_norm_g)
```

```python
import functools

import jax
import jax.numpy as jnp
from jax import lax
from jax.experimental import pallas as pl
from jax.experimental.pallas import tpu as pltpu

D_MODEL = 1024
NSA_HEADS = 8
NSA_KV_HEADS = 2
NSA_HEAD_DIM = 64
NSA_REP = NSA_HEADS // NSA_KV_HEADS
CMP_BLOCK = 32
CMP_STRIDE = 16
SEL_BLOCK = 64
N_SELECT = 16
WINDOW = 512
Q_BLOCK = 128
NSA_WIDTH = NSA_HEADS * NSA_HEAD_DIM
KV_WIDTH = NSA_KV_HEADS * NSA_HEAD_DIM
GMLP_WIDTH = 256
GMLP_GROUPS = 4
GMLP_CHUNK = 128
CONV_WIDTH = 256
CONV_TAPS = 31
XATTN_HEADS = 4
XATTN_HEAD_DIM = D_MODEL // XATTN_HEADS
D_FF = 2816
FFN_CONV_TAPS = 3
EPS = 1e-6
NEG_INF = -1e30
TINY = 1e-30
FORCE = 1e4

LANES = 128
BF16_SUBLANES = 16
VMEM_LIMIT_BYTES = 48 * 1024 * 1024

MXU_DTYPE = jnp.bfloat16
F32 = jnp.float32

IN_PROJ_ROWS = 512
SEL_KEY_CHUNK = 512
MIX_ROWS = 512
CONV_HALO = 32
XATTN_ROWS = 512
FFN_ROWS = 1024
FFN_COLS = 256
FFN_HALO = 16
GATE_PAD = LANES
KV_PACK = 2 * NSA_HEAD_DIM


def _params(*semantics):
    return pltpu.CompilerParams(dimension_semantics=semantics, vmem_limit_bytes=VMEM_LIMIT_BYTES)


def _dot(a, b):
    return jnp.dot(a.astype(MXU_DTYPE), b.astype(MXU_DTYPE), preferred_element_type=F32)


def _dot_nt(a, b):
    return lax.dot_general(a.astype(MXU_DTYPE), b.astype(MXU_DTYPE), (((1,), (1,)), ((), ())),
                           preferred_element_type=F32)


def _rms(x, g):
    return x * lax.rsqrt(jnp.mean(x * x, axis=-1, keepdims=True) + EPS) * g


def _layer_norm(x, g, b):
    mu = jnp.mean(x, axis=-1, keepdims=True)
    xc = x - mu
    var = jnp.mean(xc * xc, axis=-1, keepdims=True)
    return xc * lax.rsqrt(var + EPS) * g + b


def _gelu_tanh(x):
    return 0.5 * x * (1.0 + jnp.tanh(0.7978845608028654 * (x + 0.044715 * (x * x * x))))


def _masked_softmax(s, valid):
    s = jnp.where(valid, s, NEG_INF)
    m = jnp.max(s, axis=-1, keepdims=True)
    e = jnp.where(valid, jnp.exp(s - m), 0.0)
    return e / jnp.maximum(jnp.sum(e, axis=-1, keepdims=True), TINY)


IN_Q = (0, NSA_WIDTH)
IN_KV = (IN_Q[1], IN_Q[1] + 6 * KV_PACK)
IN_ZG = (IN_KV[1], IN_KV[1] + 2 * GMLP_WIDTH)
IN_ZC = (IN_ZG[1], IN_ZG[1] + 2 * CONV_WIDTH)
IN_GL = (IN_ZC[1], IN_ZC[1] + NSA_KV_HEADS * GATE_PAD)
IN_COLS = IN_GL[1]


def _permute_w_in(w):
    off_kv = NSA_WIDTH
    off_gl = NSA_WIDTH + 6 * KV_WIDTH
    off_zg = off_gl + NSA_HEADS * 3
    off_zc = off_zg + 2 * GMLP_WIDTH
    parts = [w[:, :NSA_WIDTH]]
    for branch in range(3):
        k0 = off_kv + (2 * branch) * KV_WIDTH
        v0 = off_kv + (2 * branch + 1) * KV_WIDTH
        for g in range(NSA_KV_HEADS):
            parts.append(w[:, k0 + g * NSA_HEAD_DIM:k0 + (g + 1) * NSA_HEAD_DIM])
            parts.append(w[:, v0 + g * NSA_HEAD_DIM:v0 + (g + 1) * NSA_HEAD_DIM])
    parts.append(w[:, off_zg:off_zg + 2 * GMLP_WIDTH])
    parts.append(w[:, off_zc:off_zc + 2 * CONV_WIDTH])
    n_gate = NSA_REP * 3
    for g in range(NSA_KV_HEADS):
        parts.append(jnp.pad(w[:, off_gl + g * n_gate:off_gl + (g + 1) * n_gate], ((0, 0), (0, GATE_PAD - n_gate))))
    return jnp.concatenate(parts, axis=1).astype(MXU_DTYPE)


def _in_proj_kernel(x_ref, g_ref, w_ref, q_ref, kv_ref, zg_ref, zc_ref, gl_ref):
    h = _rms(x_ref[...], g_ref[...]).astype(MXU_DTYPE)
    q_ref[...] = _dot(h, w_ref[:, IN_Q[0]:IN_Q[1]]).astype(q_ref.dtype)
    for j in range(6):
        c0 = IN_KV[0] + j * KV_PACK
        kv_ref[j] = _dot(h, w_ref[:, c0:c0 + KV_PACK]).astype(kv_ref.dtype)
    zg_ref[...] = _dot(h, w_ref[:, IN_ZG[0]:IN_ZG[1]]).astype(zg_ref.dtype)
    zc_ref[...] = _dot(h, w_ref[:, IN_ZC[0]:IN_ZC[1]]).astype(zc_ref.dtype)
    gl_ref[...] = _dot(h, w_ref[:, IN_GL[0]:IN_GL[1]])


def _in_proj(x, g, w):
    b, s, d = x.shape
    tm = IN_PROJ_ROWS
    return pl.pallas_call(
        _in_proj_kernel,
        grid=(b, s // tm),
        in_specs=[
            pl.BlockSpec((None, tm, d), lambda bb, i: (bb, i, 0)),
            pl.BlockSpec((1, d), lambda bb, i: (0, 0)),
            pl.BlockSpec((d, IN_COLS), lambda bb, i: (0, 0)),
        ],
        out_specs=[
            pl.BlockSpec((None, tm, NSA_WIDTH), lambda bb, i: (bb, i, 0)),
            pl.BlockSpec((None, 6, tm, KV_PACK), lambda bb, i: (bb, 0, i, 0)),
            pl.BlockSpec((None, tm, 2 * GMLP_WIDTH), lambda bb, i: (bb, i, 0)),
            pl.BlockSpec((None, tm, 2 * CONV_WIDTH), lambda bb, i: (bb, i, 0)),
            pl.BlockSpec((None, tm, NSA_KV_HEADS * GATE_PAD), lambda bb, i: (bb, i, 0)),
        ],
        out_shape=[
            jax.ShapeDtypeStruct((b, s, NSA_WIDTH), MXU_DTYPE),
            jax.ShapeDtypeStruct((b, 6, s, KV_PACK), MXU_DTYPE),
            jax.ShapeDtypeStruct((b, s, 2 * GMLP_WIDTH), MXU_DTYPE),
            jax.ShapeDtypeStruct((b, s, 2 * CONV_WIDTH), MXU_DTYPE),
            jax.ShapeDtypeStruct((b, s, NSA_KV_HEADS * GATE_PAD), F32),
        ],
        compiler_params=_params("parallel", "parallel"),
        name="in_proj",
    )(x, g.reshape(1, d), w)


CMP_ROW = CMP_STRIDE * KV_PACK


def _pack_cmp_weights(pe, w1, w2):
    dh = NSA_HEAD_DIM
    w1r = w1.reshape(2, CMP_BLOCK, dh, dh)
    z = jnp.zeros_like(w1r[0])
    wk = jnp.concatenate([w1r[0], z], axis=-1)
    wv = jnp.concatenate([z, w1r[1]], axis=-1)
    w1p = jnp.concatenate([wk, wv], axis=1).reshape(CMP_BLOCK * KV_PACK, KV_PACK)
    z2 = jnp.zeros_like(w2[0])
    w2p = jnp.concatenate([jnp.concatenate([w2[0], z2], axis=1), jnp.concatenate([z2, w2[1]], axis=1)], axis=0)
    pep = jnp.concatenate([pe[0], pe[1]], axis=-1).reshape(1, CMP_BLOCK * KV_PACK)
    pep = jnp.broadcast_to(pep, (8, CMP_BLOCK * KV_PACK))
    return pep.astype(MXU_DTYPE), w1p.astype(MXU_DTYPE), w2p.astype(MXU_DTYPE)


def _compress_kernel(c_ref, pe_ref, w1_ref, w2_ref, o_ref):
    c = c_ref[...]
    n = c.shape[0]
    first = _dot(c, w1_ref[:CMP_ROW, :])
    second = _dot(c, w1_ref[CMP_ROW:, :])
    pe_term = _dot(pe_ref[...], w1_ref[...])[0:1, :]
    h = first + pltpu.roll(second, n - 1, 0) + pe_term
    o_ref[...] = _dot(_gelu_tanh(h), w2_ref[...]).astype(o_ref.dtype)


def _compress(kv6, pep, w1p, w2p):
    b, _, s, _ = kv6.shape
    n = s // CMP_STRIDE
    c = kv6.reshape(b, 6, n, CMP_ROW)
    return pl.pallas_call(
        _compress_kernel,
        grid=(b, NSA_KV_HEADS),
        in_specs=[
            pl.BlockSpec((None, None, n, CMP_ROW), lambda bb, g: (bb, g, 0, 0)),
            pl.BlockSpec((8, 2 * CMP_ROW), lambda bb, g: (0, 0)),
            pl.BlockSpec((2 * CMP_ROW, KV_PACK), lambda bb, g: (0, 0)),
            pl.BlockSpec((KV_PACK, KV_PACK), lambda bb, g: (0, 0)),
        ],
        out_specs=pl.BlockSpec((None, None, n, KV_PACK), lambda bb, g: (bb, g, 0, 0)),
        out_shape=jax.ShapeDtypeStruct((b, NSA_KV_HEADS, n, KV_PACK), MXU_DTYPE),
        compiler_params=_params("parallel", "parallel"),
        name="nsa_compress",
    )(c, pep, w1p, w2p)


def _split3(x):
    a = x.astype(MXU_DTYPE)
    r = x - a.astype(F32)
    b = r.astype(MXU_DTYPE)
    c = (r - b.astype(F32)).astype(MXU_DTYPE)
    return a, b, c


def _nsa_kernel(q_ref, gl_ref, cmp_ref, kvs_ref, kvw_ref, o_ref, *, seq):
    dh, qb, rep = NSA_HEAD_DIM, Q_BLOCK, NSA_REP
    n_cmp_rows = seq // CMP_STRIDE
    n_slc = seq // SEL_BLOCK
    k_sel = min(N_SELECT, n_slc)
    kc = min(SEL_KEY_CHUNK, seq)
    win_keys = WINDOW + qb
    scale = dh ** -0.5

    bi = pl.program_id(2)
    t0 = bi * qb
    t_col = t0 + lax.broadcasted_iota(jnp.int32, (qb, 1), 0)

    lane = lax.broadcasted_iota(jnp.int32, (qb, KV_PACK), 1)
    heads = []
    for p in range(rep // 2):
        pair = q_ref[:, p * KV_PACK:(p + 1) * KV_PACK].astype(F32) * scale
        heads.append(jnp.where(lane < dh, pair, 0.0))
        heads.append(jnp.where(lane < dh, pltpu.roll(pair, dh, 1), 0.0))
    q4 = jnp.concatenate(heads, axis=0).astype(MXU_DTYPE)

    kvc = cmp_ref[...]
    s_c = _dot_nt(q4, kvc).reshape(rep, qb, n_cmp_rows)
    n_idx = lax.broadcasted_iota(jnp.int32, (1, n_cmp_rows), 1)
    valid_c = (n_idx * CMP_STRIDE + (CMP_BLOCK - 1) <= t_col) & (n_idx < n_cmp_rows - 1)
    p_c = _masked_softmax(s_c, valid_c[None])
    o_c = _dot(p_c.reshape(rep * qb, n_cmp_rows), kvc)

    p_sum = p_c[0]
    for r in range(1, rep):
        p_sum = p_sum + p_c[r]
    ci = lax.broadcasted_iota(jnp.int32, (n_cmp_rows, n_slc), 0) * CMP_STRIDE
    sj = lax.broadcasted_iota(jnp.int32, (n_cmp_rows, n_slc), 1) * SEL_BLOCK
    overlap = ((ci < sj + SEL_BLOCK) & (ci + CMP_BLOCK > sj)).astype(MXU_DTYPE)
    imp = None
    for term in _split3(p_sum):
        d = jnp.dot(term, overlap, preferred_element_type=F32)
        imp = d if imp is None else imp + d
    jj = lax.broadcasted_iota(jnp.int32, (qb, n_slc), 1)
    cur = t_col // SEL_BLOCK
    forced = (jj == 0) | (jj == cur) | (jj == cur - 1)
    imp = jnp.where(forced, FORCE, imp)
    imp = jnp.where(jj <= cur, imp, -FORCE)
    jf = jj.astype(F32)
    sel = jnp.zeros((qb, n_slc), F32)
    for _ in range(k_sel):
        mx = jnp.max(imp, axis=-1, keepdims=True)
        first = jnp.min(jnp.where(imp == mx, jf, float(n_slc)), axis=-1, keepdims=True)
        pick = jf == first
        sel = jnp.where(pick, 1.0, sel)
        imp = jnp.where(pick, -3.0e38, imp)
    sel_b = sel.astype(MXU_DTYPE)

    blk_row = lax.broadcasted_iota(jnp.int32, (n_slc, kc), 0)
    blk_of_lane = lax.broadcasted_iota(jnp.int32, (n_slc, kc), 1) // SEL_BLOCK
    key_lane = lax.broadcasted_iota(jnp.int32, (1, kc), 1)

    def sel_body(c, carry):
        m, l, acc = carry
        k0 = pl.multiple_of(c * kc, kc)
        kv = kvs_ref[pl.ds(k0, kc), :]
        s = _dot_nt(q4, kv).reshape(rep, qb, kc)
        expand = (blk_row == blk_of_lane + c * (kc // SEL_BLOCK)).astype(MXU_DTYPE)
        chosen = jnp.dot(sel_b, expand, preferred_element_type=F32)
        valid = ((chosen > 0.5) & (key_lane + k0 <= t_col))[None]
        s = jnp.where(valid, s, NEG_INF)
        m_new = jnp.maximum(m, jnp.max(s, axis=-1, keepdims=True))
        alpha = jnp.exp(m - m_new)
        e = jnp.where(valid, jnp.exp(s - m_new), 0.0)
        l = alpha * l + jnp.sum(e, axis=-1, keepdims=True)
        pv = _dot(e.reshape(rep * qb, kc), kv).reshape(rep, qb, KV_PACK)
        return m_new, l, alpha * acc + pv

    n_chunks = (t0 + qb + kc - 1) // kc
    init = (jnp.full((rep, qb, 1), NEG_INF, F32), jnp.zeros((rep, qb, 1), F32),
            jnp.zeros((rep, qb, KV_PACK), F32))
    _, l_s, acc_s = lax.fori_loop(0, n_chunks, sel_body, init)
    o_s = (acc_s / jnp.maximum(l_s, TINY)).reshape(rep * qb, KV_PACK)

    w0 = pl.multiple_of(jnp.maximum(t0 - WINDOW, 0), qb)
    kvw = kvw_ref[pl.ds(w0, win_keys), :]
    s_w = _dot_nt(q4, kvw).reshape(rep, qb, win_keys)
    kp = w0 + lax.broadcasted_iota(jnp.int32, (1, win_keys), 1)
    valid_w = (kp <= t_col) & (kp > t_col - WINDOW)
    p_w = _masked_softmax(s_w, valid_w[None])
    o_w = _dot(p_w.reshape(rep * qb, win_keys), kvw)

    gates = jax.nn.sigmoid(gl_ref[...])
    outs = []
    for r in range(rep):
        rows = slice(r * qb, (r + 1) * qb)
        outs.append(gates[:, 3 * r:3 * r + 1] * o_c[rows] + gates[:, 3 * r + 1:3 * r + 2] * o_s[rows]
                    + gates[:, 3 * r + 2:3 * r + 3] * o_w[rows])
    for p in range(rep // 2):
        pair = jnp.where(lane < dh, pltpu.roll(outs[2 * p], dh, 1), outs[2 * p + 1])
        o_ref[:, p * KV_PACK:(p + 1) * KV_PACK] = pair.astype(o_ref.dtype)


def _nsa(q, gl, kv_cmp, kv6):
    b, s, _ = q.shape
    nb = s // Q_BLOCK
    gw = NSA_REP * NSA_HEAD_DIM
    n_cmp_rows = s // CMP_STRIDE
    return pl.pallas_call(
        functools.partial(_nsa_kernel, seq=s),
        grid=(b, NSA_KV_HEADS, nb),
        in_specs=[
            pl.BlockSpec((None, Q_BLOCK, gw), lambda bb, g, i: (bb, i, g)),
            pl.BlockSpec((None, Q_BLOCK, GATE_PAD), lambda bb, g, i: (bb, i, g)),
            pl.BlockSpec((None, None, n_cmp_rows, KV_PACK), lambda bb, g, i: (bb, g, 0, 0)),
            pl.BlockSpec((None, None, s, KV_PACK), lambda bb, g, i: (bb, 2 + g, 0, 0)),
            pl.BlockSpec((None, None, s, KV_PACK), lambda bb, g, i: (bb, 4 + g, 0, 0)),
        ],
        out_specs=pl.BlockSpec((None, Q_BLOCK, gw), lambda bb, g, i: (bb, i, g)),
        out_shape=jax.ShapeDtypeStruct((b, s, NSA_WIDTH), MXU_DTYPE),
        compiler_params=_params("parallel", "parallel", "arbitrary"),
        name="nsa_attention",
    )(q, gl, kv_cmp, kv6, kv6)


def _mix_out_kernel(x_ref, nsa_ref, zg_ref, zc_ref, zch_ref, gln_g_ref, gln_b_ref, ws_ref, bs_ref,
                    dww_ref, dwb_ref, cln_g_ref, cln_b_ref, og_ref, wo_ref, o_ref, hbuf_ref):
    tm = x_ref.shape[0]
    i = pl.program_id(1)

    def glu(z):
        z = z.astype(F32)
        return z[:, :CONV_WIDTH] * jax.nn.sigmoid(z[:, CONV_WIDTH:])

    halo = glu(zch_ref[...])
    hbuf_ref[0:CONV_HALO, :] = jnp.where(i == 0, 0.0, halo)
    hbuf_ref[CONV_HALO:CONV_HALO + tm, :] = glu(zc_ref[...])
    conv = jnp.broadcast_to(dwb_ref[...], (tm, CONV_WIDTH))
    for k in range(CONV_TAPS):
        off = CONV_HALO - (CONV_TAPS - 1) + k
        conv = conv + dww_ref[k:k + 1, :] * hbuf_ref[off:off + tm, :]
    y_conv = jax.nn.silu(_layer_norm(conv, cln_g_ref[...], cln_b_ref[...]))

    zg = _gelu_tanh(zg_ref[...].astype(F32))
    u = zg[:, :GMLP_WIDTH]
    v = _layer_norm(zg[:, GMLP_WIDTH:], gln_g_ref[...], gln_b_ref[...]).astype(MXU_DTYPE)
    row = lax.broadcasted_iota(jnp.int32, (GMLP_CHUNK, GMLP_CHUNK), 0)
    col = lax.broadcasted_iota(jnp.int32, (GMLP_CHUNK, GMLP_CHUNK), 1)
    grp = lax.broadcasted_iota(jnp.int32, (GMLP_CHUNK, GMLP_WIDTH), 1) // (GMLP_WIDTH // GMLP_GROUPS)
    w_causal = [jnp.where(row >= col, ws_ref[g], 0.0).astype(MXU_DTYPE) for g in range(GMLP_GROUPS)]
    bias = jnp.zeros((GMLP_CHUNK, GMLP_WIDTH), F32)
    for g in range(GMLP_GROUPS):
        bias = jnp.where(grp == g, bs_ref[:, g:g + 1], bias)
    mixed = []
    for c in range(tm // GMLP_CHUNK):
        vc = v[c * GMLP_CHUNK:(c + 1) * GMLP_CHUNK]
        acc = bias
        for g in range(GMLP_GROUPS):
            acc = acc + jnp.where(grp == g, jnp.dot(w_causal[g], vc, preferred_element_type=F32), 0.0)
        mixed.append(acc)
    y_gmlp = u * jnp.concatenate(mixed, axis=0)

    og = og_ref[...]
    n0, n1 = NSA_WIDTH, NSA_WIDTH + GMLP_WIDTH
    y = _dot(_rms(nsa_ref[...].astype(F32), og[:, :n0]), wo_ref[:n0, :])
    y = y + _dot(_rms(y_gmlp, og[:, n0:n1]), wo_ref[n0:n1, :])
    y = y + _dot(_rms(y_conv, og[:, n1:]), wo_ref[n1:, :])
    o_ref[...] = x_ref[...] + y


def _mix_out(x, y_nsa, zg, zc, p):
    b, s, d = x.shape
    tm = min(MIX_ROWS, s)
    halo_blocks = tm // CONV_HALO
    row2 = lambda a: a.reshape(1, -1)
    full = lambda shape: pl.BlockSpec(shape, lambda bb, i: (0,) * len(shape))
    tile = lambda w: pl.BlockSpec((None, tm, w), lambda bb, i: (bb, i, 0))
    return pl.pallas_call(
        _mix_out_kernel,
        grid=(b, s // tm),
        in_specs=[
            tile(d), tile(NSA_WIDTH), tile(2 * GMLP_WIDTH), tile(2 * CONV_WIDTH),
            pl.BlockSpec((None, CONV_HALO, 2 * CONV_WIDTH),
                         lambda bb, i: (bb, jnp.maximum(i * halo_blocks - 1, 0), 0)),
            full((1, GMLP_WIDTH)), full((1, GMLP_WIDTH)),
            full((GMLP_GROUPS, GMLP_CHUNK, GMLP_CHUNK)), full((GMLP_CHUNK, GMLP_GROUPS)),
            full((CONV_TAPS, CONV_WIDTH)), full((1, CONV_WIDTH)), full((1, CONV_WIDTH)), full((1, CONV_WIDTH)),
            full((1, d)), full((d, d)),
        ],
        out_specs=tile(d),
        out_shape=jax.ShapeDtypeStruct((b, s, d), F32),
        scratch_shapes=[pltpu.VMEM((CONV_HALO + tm, CONV_WIDTH), F32)],
        compiler_params=_params("parallel", "parallel"),
        name="mix_out",
    )(x, y_nsa, zg, zc, zc, row2(p["gmlp_ln_g"]), row2(p["gmlp_ln_b"]), p["gmlp_ws"], p["gmlp_bs"].T,
      p["conv_dw_w"], row2(p["conv_dw_b"]), row2(p["conv_ln_g"]), row2(p["conv_ln_b"]),
      row2(p["mix_out_g"]), p["w_out"].astype(MXU_DTYPE))


def _mem_kv_kernel(m_ref, g_ref, wk_ref, wv_ref, k_ref, v_ref):
    m = _rms(m_ref[...], g_ref[...]).astype(MXU_DTYPE)
    k_ref[...] = _dot(m, wk_ref[...]).astype(k_ref.dtype)
    v_ref[...] = _dot(m, wv_ref[...]).astype(v_ref.dtype)


def _mem_kv(mem, g, wk, wv):
    b, n, d = mem.shape
    blk = pl.BlockSpec((None, n, d), lambda bb: (bb, 0, 0))
    wspec = pl.BlockSpec((d, d), lambda bb: (0, 0))
    return pl.pallas_call(
        _mem_kv_kernel,
        grid=(b,),
        in_specs=[blk, pl.BlockSpec((1, d), lambda bb: (0, 0)), wspec, wspec],
        out_specs=[blk, blk],
        out_shape=[jax.ShapeDtypeStruct((b, n, d), MXU_DTYPE)] * 2,
        compiler_params=_params("parallel"),
        name="xattn_mem_kv",
    )(mem, g.reshape(1, d), wk.astype(MXU_DTYPE), wv.astype(MXU_DTYPE))


def _xattn_kernel(x_ref, g_ref, wq_ref, k_ref, v_ref, wo_ref, o_ref):
    dh = XATTN_HEAD_DIM
    x = x_ref[...]
    h = _rms(x, g_ref[...]).astype(MXU_DTYPE)
    y = x
    for hd in range(XATTN_HEADS):
        cols = slice(hd * dh, (hd + 1) * dh)
        q = _dot(h, wq_ref[:, cols]) * (dh ** -0.5)
        s = _dot_nt(q, k_ref[:, cols])
        e = jnp.exp(s - jnp.max(s, axis=-1, keepdims=True))
        p = e / jnp.sum(e, axis=-1, keepdims=True)
        y = y + _dot(_dot(p, v_ref[:, cols]), wo_ref[cols, :])
    o_ref[...] = y


def _xattn(x, g, wq, k, v, wo):
    b, s, d = x.shape
    n = k.shape[1]
    tm = min(XATTN_ROWS, s)
    tile = pl.BlockSpec((None, tm, d), lambda bb, i: (bb, i, 0))
    wspec = pl.BlockSpec((d, d), lambda bb, i: (0, 0))
    mspec = pl.BlockSpec((None, n, d), lambda bb, i: (bb, 0, 0))
    return pl.pallas_call(
        _xattn_kernel,
        grid=(b, s // tm),
        in_specs=[tile, pl.BlockSpec((1, d), lambda bb, i: (0, 0)), wspec, mspec, mspec, wspec],
        out_specs=tile,
        out_shape=jax.ShapeDtypeStruct((b, s, d), F32),
        compiler_params=_params("parallel", "parallel"),
        name="xattn",
    )(x, g.reshape(1, d), wq.astype(MXU_DTYPE), k, v, wo.astype(MXU_DTYPE))


def _ffn_kernel(x_ref, xh_ref, g_ref, wg_ref, wu_ref, cwg_ref, cwu_ref, cbg_ref, cbu_ref, wd_ref, fg_ref,
                o_ref, xn_ref, acc_ref, *, final_norm):
    tm = x_ref.shape[0]
    i, j = pl.program_id(1), pl.program_id(2)

    @pl.when(j == 0)
    def _():
        halo = jnp.where(i == 0, 0.0, xh_ref[...])
        xn_ref[0:FFN_HALO, :] = _rms(halo, g_ref[...]).astype(xn_ref.dtype)
        xn_ref[FFN_HALO:FFN_HALO + tm, :] = _rms(x_ref[...], g_ref[...]).astype(xn_ref.dtype)
        acc_ref[...] = jnp.zeros_like(acc_ref)

    xn = xn_ref[...]

    def conv(h, w_ref, b_ref):
        y = w_ref[2:3, :] * h + w_ref[1:2, :] * pltpu.roll(h, 1, 0) + w_ref[0:1, :] * pltpu.roll(h, 2, 0)
        return y[FFN_HALO:, :] + b_ref[...]

    gate = conv(_dot(xn, wg_ref[...]), cwg_ref, cbg_ref)
    up = conv(_dot(xn, wu_ref[...]), cwu_ref, cbu_ref)
    acc_ref[...] += _dot(jax.nn.silu(gate) * up, wd_ref[...])

    @pl.when(j == pl.num_programs(2) - 1)
    def _():
        y = x_ref[...] + acc_ref[...]
        o_ref[...] = _rms(y, fg_ref[...]) if final_norm else y


def _ffn(x, g, w_up, dw_w, dw_b, w_down, final_g, final_norm):
    b, s, d = x.shape
    tm = min(FFN_ROWS, s)
    tc = FFN_COLS
    nj = D_FF // tc
    halo_blocks = tm // FFN_HALO
    dw_b = dw_b.reshape(1, -1)
    tile = pl.BlockSpec((None, tm, d), lambda bb, i, j: (bb, i, 0))
    vec = pl.BlockSpec((1, d), lambda bb, i, j: (0, 0))
    return pl.pallas_call(
        functools.partial(_ffn_kernel, final_norm=final_norm),
        grid=(b, s // tm, nj),
        in_specs=[
            tile,
            pl.BlockSpec((None, FFN_HALO, d), lambda bb, i, j: (bb, jnp.maximum(i * halo_blocks - 1, 0), 0)),
            vec,
            pl.BlockSpec((d, tc), lambda bb, i, j: (0, j)),
            pl.BlockSpec((d, tc), lambda bb, i, j: (0, nj + j)),
            pl.BlockSpec((FFN_CONV_TAPS, tc), lambda bb, i, j: (0, j)),
            pl.BlockSpec((FFN_CONV_TAPS, tc), lambda bb, i, j: (0, nj + j)),
            pl.BlockSpec((1, tc), lambda bb, i, j: (0, j)),
            pl.BlockSpec((1, tc), lambda bb, i, j: (0, nj + j)),
            pl.BlockSpec((tc, d), lambda bb, i, j: (j, 0)),
            vec,
        ],
        out_specs=tile,
        out_shape=jax.ShapeDtypeStruct((b, s, d), F32),
        scratch_shapes=[pltpu.VMEM((FFN_HALO + tm, d), MXU_DTYPE), pltpu.VMEM((tm, d), F32)],
        compiler_params=_params("parallel", "parallel", "arbitrary"),
        name="conv_ffn",
    )(x, x, g.reshape(1, d), w_up.astype(MXU_DTYPE), w_up.astype(MXU_DTYPE), dw_w, dw_w, dw_b, dw_b,
      w_down.astype(MXU_DTYPE), final_g.reshape(1, d))


def kernel(x, mem, mix_norm_g, w_in, cmp_pe, cmp_w1, cmp_w2, gmlp_ln_g, gmlp_ln_b, gmlp_ws, gmlp_bs, conv_dw_w, conv_dw_b, conv_ln_g, conv_ln_b, mix_out_g, w_out, xattn_norm_g, mem_norm_g, xattn_wq, xattn_wk, xattn_wv, xattn_wo, ffn_norm_g, ffn_w_up, ffn_dw_w, ffn_dw_b, ffn_w_down, final_norm_g):
    depth = w_in.shape[0]
    for l in range(depth):
        q, kv6, zg, zc, gl = _in_proj(x, mix_norm_g[l], _permute_w_in(w_in[l]))
        kv_cmp = _compress(kv6, *_pack_cmp_weights(cmp_pe[l], cmp_w1[l], cmp_w2[l]))
        y_nsa = _nsa(q, gl, kv_cmp, kv6)
        mix = dict(gmlp_ln_g=gmlp_ln_g[l], gmlp_ln_b=gmlp_ln_b[l], gmlp_ws=gmlp_ws[l], gmlp_bs=gmlp_bs[l],
                   conv_dw_w=conv_dw_w[l], conv_dw_b=conv_dw_b[l], conv_ln_g=conv_ln_g[l],
                   conv_ln_b=conv_ln_b[l], mix_out_g=mix_out_g[l], w_out=w_out[l])
        x = _mix_out(x, y_nsa, zg, zc, mix)
        mk, mv = _mem_kv(mem, mem_norm_g[l], xattn_wk[l], xattn_wv[l])
        x = _xattn(x, xattn_norm_g[l], xattn_wq[l], mk, mv, xattn_wo[l])
        x = _ffn(x, ffn_norm_g[l], ffn_w_up[l], ffn_dw_w[l], ffn_dw_b[l], ffn_w_down[l], final_norm_g,
                 final_norm=(l == depth - 1))
    return x
```

```python
import functools

import jax
import jax.numpy as jnp
from jax import lax
from jax.experimental import pallas as pl
from jax.experimental.pallas import tpu as pltpu

D_MODEL = 1024
NSA_HEADS = 8
NSA_KV_HEADS = 2
NSA_HEAD_DIM = 64
NSA_REP = NSA_HEADS // NSA_KV_HEADS
CMP_BLOCK = 32
CMP_STRIDE = 16
SEL_BLOCK = 64
N_SELECT = 16
WINDOW = 512
Q_BLOCK = 128
NSA_WIDTH = NSA_HEADS * NSA_HEAD_DIM
KV_WIDTH = NSA_KV_HEADS * NSA_HEAD_DIM
GMLP_WIDTH = 256
GMLP_GROUPS = 4
GMLP_CHUNK = 128
CONV_WIDTH = 256
CONV_TAPS = 31
XATTN_HEADS = 4
XATTN_HEAD_DIM = D_MODEL // XATTN_HEADS
D_FF = 2816
FFN_CONV_TAPS = 3
EPS = 1e-6
NEG_INF = -1e30
TINY = 1e-30
FORCE = 1e4
LOG2E = 1.4426950408889634
MAX_FLOOR = -1e29
KNOCKOUT = -3.0e38

LANES = 128
BF16_SUBLANES = 16
VMEM_LIMIT_BYTES = 48 * 1024 * 1024

MXU_DTYPE = jnp.bfloat16
F32 = jnp.float32

IN_PROJ_ROWS = 512
SEL_KEY_CHUNK = 512
MIX_ROWS = 512
CONV_HALO = 32
XATTN_ROWS = 512
FFN_ROWS = 1024
FFN_COLS = 256
FFN_HALO = 16
GATE_PAD = LANES
KV_PACK = 2 * NSA_HEAD_DIM


def _params(*semantics):
    return pltpu.CompilerParams(dimension_semantics=semantics, vmem_limit_bytes=VMEM_LIMIT_BYTES)


def _dot(a, b):
    return jnp.dot(a.astype(MXU_DTYPE), b.astype(MXU_DTYPE), preferred_element_type=F32)


def _dot_nt(a, b):
    return lax.dot_general(a.astype(MXU_DTYPE), b.astype(MXU_DTYPE), (((1,), (1,)), ((), ())),
                           preferred_element_type=F32)


def _rms(x, g):
    return x * lax.rsqrt(jnp.mean(x * x, axis=-1, keepdims=True) + EPS) * g


def _layer_norm(x, g, b):
    mu = jnp.mean(x, axis=-1, keepdims=True)
    xc = x - mu
    var = jnp.mean(xc * xc, axis=-1, keepdims=True)
    return xc * lax.rsqrt(var + EPS) * g + b


def _gelu_tanh(x):
    return 0.5 * x * (1.0 + jnp.tanh(0.7978845608028654 * (x + 0.044715 * (x * x * x))))


def _masked_softmax(s, valid):
    s = jnp.where(valid, s, NEG_INF)
    m = jnp.max(s, axis=-1, keepdims=True)
    e = jnp.where(valid, jnp.exp(s - m), 0.0)
    return e / jnp.maximum(jnp.sum(e, axis=-1, keepdims=True), TINY)


IN_Q = (0, NSA_WIDTH)
IN_KV = (IN_Q[1], IN_Q[1] + 6 * KV_PACK)
IN_ZG = (IN_KV[1], IN_KV[1] + 2 * GMLP_WIDTH)
IN_ZC = (IN_ZG[1], IN_ZG[1] + 2 * CONV_WIDTH)
IN_GL = (IN_ZC[1], IN_ZC[1] + NSA_KV_HEADS * GATE_PAD)
IN_COLS = IN_GL[1]


def _permute_w_in(w):
    off_kv = NSA_WIDTH
    off_gl = NSA_WIDTH + 6 * KV_WIDTH
    off_zg = off_gl + NSA_HEADS * 3
    off_zc = off_zg + 2 * GMLP_WIDTH
    parts = [w[:, :NSA_WIDTH]]
    for branch in range(3):
        k0 = off_kv + (2 * branch) * KV_WIDTH
        v0 = off_kv + (2 * branch + 1) * KV_WIDTH
        for g in range(NSA_KV_HEADS):
            parts.append(w[:, k0 + g * NSA_HEAD_DIM:k0 + (g + 1) * NSA_HEAD_DIM])
            parts.append(w[:, v0 + g * NSA_HEAD_DIM:v0 + (g + 1) * NSA_HEAD_DIM])
    parts.append(w[:, off_zg:off_zg + 2 * GMLP_WIDTH])
    parts.append(w[:, off_zc:off_zc + 2 * CONV_WIDTH])
    n_gate = NSA_REP * 3
    for g in range(NSA_KV_HEADS):
        parts.append(jnp.pad(w[:, off_gl + g * n_gate:off_gl + (g + 1) * n_gate], ((0, 0), (0, GATE_PAD - n_gate))))
    return jnp.concatenate(parts, axis=1).astype(MXU_DTYPE)


def _in_proj_kernel(x_ref, g_ref, w_ref, q_ref, kv_ref, zg_ref, zc_ref, gl_ref):
    h = _rms(x_ref[...], g_ref[...]).astype(MXU_DTYPE)
    q_ref[...] = _dot(h, w_ref[:, IN_Q[0]:IN_Q[1]]).astype(q_ref.dtype)
    for j in range(6):
        c0 = IN_KV[0] + j * KV_PACK
        kv_ref[j] = _dot(h, w_ref[:, c0:c0 + KV_PACK]).astype(kv_ref.dtype)
    zg_ref[...] = _dot(h, w_ref[:, IN_ZG[0]:IN_ZG[1]]).astype(zg_ref.dtype)
    zc_ref[...] = _dot(h, w_ref[:, IN_ZC[0]:IN_ZC[1]]).astype(zc_ref.dtype)
    gl_ref[...] = _dot(h, w_ref[:, IN_GL[0]:IN_GL[1]])


def _in_proj(x, g, w):
    b, s, d = x.shape
    tm = IN_PROJ_ROWS
    return pl.pallas_call(
        _in_proj_kernel,
        grid=(b, s // tm),
        in_specs=[
            pl.BlockSpec((None, tm, d), lambda bb, i: (bb, i, 0)),
            pl.BlockSpec((1, d), lambda bb, i: (0, 0)),
            pl.BlockSpec((d, IN_COLS), lambda bb, i: (0, 0)),
        ],
        out_specs=[
            pl.BlockSpec((None, tm, NSA_WIDTH), lambda bb, i: (bb, i, 0)),
            pl.BlockSpec((None, 6, tm, KV_PACK), lambda bb, i: (bb, 0, i, 0)),
            pl.BlockSpec((None, tm, 2 * GMLP_WIDTH), lambda bb, i: (bb, i, 0)),
            pl.BlockSpec((None, tm, 2 * CONV_WIDTH), lambda bb, i: (bb, i, 0)),
            pl.BlockSpec((None, tm, NSA_KV_HEADS * GATE_PAD), lambda bb, i: (bb, i, 0)),
        ],
        out_shape=[
            jax.ShapeDtypeStruct((b, s, NSA_WIDTH), MXU_DTYPE),
            jax.ShapeDtypeStruct((b, 6, s, KV_PACK), MXU_DTYPE),
            jax.ShapeDtypeStruct((b, s, 2 * GMLP_WIDTH), MXU_DTYPE),
            jax.ShapeDtypeStruct((b, s, 2 * CONV_WIDTH), MXU_DTYPE),
            jax.ShapeDtypeStruct((b, s, NSA_KV_HEADS * GATE_PAD), F32),
        ],
        compiler_params=_params("parallel", "parallel"),
        name="in_proj",
    )(x, g.reshape(1, d), w)


CMP_ROW = CMP_STRIDE * KV_PACK


def _pack_cmp_weights(pe, w1, w2):
    dh = NSA_HEAD_DIM
    w1r = w1.reshape(2, CMP_BLOCK, dh, dh)
    z = jnp.zeros_like(w1r[0])
    wk = jnp.concatenate([w1r[0], z], axis=-1)
    wv = jnp.concatenate([z, w1r[1]], axis=-1)
    w1p = jnp.concatenate([wk, wv], axis=1).reshape(CMP_BLOCK * KV_PACK, KV_PACK)
    z2 = jnp.zeros_like(w2[0])
    w2p = jnp.concatenate([jnp.concatenate([w2[0], z2], axis=1), jnp.concatenate([z2, w2[1]], axis=1)], axis=0)
    pep = jnp.concatenate([pe[0], pe[1]], axis=-1).reshape(1, CMP_BLOCK * KV_PACK)
    pep = jnp.broadcast_to(pep, (8, CMP_BLOCK * KV_PACK))
    return pep.astype(MXU_DTYPE), w1p.astype(MXU_DTYPE), w2p.astype(MXU_DTYPE)


def _compress_kernel(c_ref, pe_ref, w1_ref, w2_ref, o_ref):
    c = c_ref[...]
    n = c.shape[0]
    first = _dot(c, w1_ref[:CMP_ROW, :])
    second = _dot(c, w1_ref[CMP_ROW:, :])
    pe_term = _dot(pe_ref[...], w1_ref[...])[0:1, :]
    h = first + pltpu.roll(second, n - 1, 0) + pe_term
    o_ref[...] = _dot(_gelu_tanh(h), w2_ref[...]).astype(o_ref.dtype)


def _compress(kv6, pep, w1p, w2p):
    b, _, s, _ = kv6.shape
    n = s // CMP_STRIDE
    c = kv6.reshape(b, 6, n, CMP_ROW)
    return pl.pallas_call(
        _compress_kernel,
        grid=(b, NSA_KV_HEADS),
        in_specs=[
            pl.BlockSpec((None, None, n, CMP_ROW), lambda bb, g: (bb, g, 0, 0)),
            pl.BlockSpec((8, 2 * CMP_ROW), lambda bb, g: (0, 0)),
            pl.BlockSpec((2 * CMP_ROW, KV_PACK), lambda bb, g: (0, 0)),
            pl.BlockSpec((KV_PACK, KV_PACK), lambda bb, g: (0, 0)),
        ],
        out_specs=pl.BlockSpec((None, None, n, KV_PACK), lambda bb, g: (bb, g, 0, 0)),
        out_shape=jax.ShapeDtypeStruct((b, NSA_KV_HEADS, n, KV_PACK), MXU_DTYPE),
        compiler_params=_params("parallel", "parallel"),
        name="nsa_compress",
    )(c, pep, w1p, w2p)


def _split3(x):
    a = x.astype(MXU_DTYPE)
    r = x - a.astype(F32)
    b = r.astype(MXU_DTYPE)
    c = (r - b.astype(F32)).astype(MXU_DTYPE)
    return a, b, c


def _softmax2_parts(s):
    m = jnp.maximum(jnp.max(s, axis=-1, keepdims=True), MAX_FLOOR)
    e = jnp.exp2(s - m)
    return e, 1.0 / jnp.maximum(jnp.sum(e, axis=-1, keepdims=True), TINY)


def _nsa_kernel(q_ref, gl_ref, cmp_ref, kvs_ref, kvw_ref, blk_ref, o_ref, *, seq):
    dh, qb, rep = NSA_HEAD_DIM, Q_BLOCK, NSA_REP
    n_cmp_rows = seq // CMP_STRIDE
    n_slc = seq // SEL_BLOCK
    k_sel = min(N_SELECT, n_slc)
    kc = min(SEL_KEY_CHUNK, seq)
    win_keys = WINDOW + qb
    sel_shift = SEL_BLOCK.bit_length() - 1

    bi = pl.program_id(2)
    t0 = bi * qb
    t_col = t0 + lax.broadcasted_iota(jnp.int32, (qb, 1), 0)
    t_row = t0 + lax.broadcasted_iota(jnp.int32, (1, qb), 1)

    lane = lax.broadcasted_iota(jnp.int32, (qb, KV_PACK), 1)
    heads = []
    for p in range(rep // 2):
        pair = q_ref[:, p * KV_PACK:(p + 1) * KV_PACK].astype(F32) * (dh ** -0.5 * LOG2E)
        heads.append(jnp.where(lane < dh, pair, 0.0))
        heads.append(jnp.where(lane < dh, pltpu.roll(pair, dh, 1), 0.0))
    q4 = jnp.concatenate(heads, axis=0).astype(MXU_DTYPE)

    kvc = cmp_ref[...]
    n_idx = lax.broadcasted_iota(jnp.int32, (1, n_cmp_rows), 1)
    valid_c = (n_idx * CMP_STRIDE + (CMP_BLOCK - 1) <= t_col) & (n_idx < n_cmp_rows - 1)
    bias_c = jnp.where(valid_c, 0.0, NEG_INF)
    e_c, inv_c = _softmax2_parts(_dot_nt(q4, kvc).reshape(rep, qb, n_cmp_rows) + bias_c[None])
    p_c = e_c * inv_c
    o_c = _dot(p_c.reshape(rep * qb, n_cmp_rows), kvc)

    p_sum = p_c[0]
    for r in range(1, rep):
        p_sum = p_sum + p_c[r]
    sj = lax.broadcasted_iota(jnp.int32, (n_slc, n_cmp_rows), 0) * SEL_BLOCK
    ci = lax.broadcasted_iota(jnp.int32, (n_slc, n_cmp_rows), 1) * CMP_STRIDE
    overlap_t = ((ci < sj + SEL_BLOCK) & (ci + CMP_BLOCK > sj)).astype(MXU_DTYPE)
    imp = None
    for term in _split3(p_sum):
        d = _dot_nt(overlap_t, term)
        imp = d if imp is None else imp + d
    jj = lax.broadcasted_iota(jnp.int32, (n_slc, qb), 0)
    cur = t_row >> sel_shift
    forced = (jj == 0) | (jj == cur) | (jj == cur - 1)
    imp = jnp.where(forced, FORCE, imp)
    imp = jnp.where(jj <= cur, imp, -FORCE)
    jf = jj.astype(F32)
    sel = jnp.zeros((n_slc, qb), F32)
    for _ in range(k_sel):
        mx = jnp.max(imp, axis=0, keepdims=True)
        first = jnp.min(jnp.where(imp == mx, jf, float(n_slc)), axis=0, keepdims=True)
        pick = jf == first
        sel = jnp.where(pick, 1.0, sel)
        imp = jnp.where(pick, KNOCKOUT, imp)
    bias_t = jnp.where((sel > 0.5) & (jj < cur), 0.0, NEG_INF)
    neg_sel = bias_t.T.astype(MXU_DTYPE)
    neg_sel4 = jnp.concatenate([neg_sel] * rep, axis=0)

    w0 = pl.multiple_of(jnp.maximum(t0 - WINDOW, 0), qb)
    kvw = kvw_ref[pl.ds(w0, win_keys), :]
    kp = w0 + lax.broadcasted_iota(jnp.int32, (1, win_keys), 1)
    bias_w = jnp.where((kp <= t_col) & (kp > t_col - WINDOW), 0.0, NEG_INF)
    e_w, inv_w = _softmax2_parts(_dot_nt(q4, kvw).reshape(rep, qb, win_keys) + bias_w[None])
    o_w = _dot(e_w.reshape(rep * qb, win_keys), kvw) * inv_w.reshape(rep * qb, 1)

    ii = lax.broadcasted_iota(jnp.int32, (qb, qb), 0)
    kk = lax.broadcasted_iota(jnp.int32, (qb, qb), 1)
    bias_d = jnp.where(((ii >> sel_shift) == (kk >> sel_shift)) & (kk <= ii), 0.0, NEG_INF)
    kvd = kvs_ref[pl.ds(pl.multiple_of(t0, qb), qb), :]
    s_d = _dot_nt(q4, kvd).reshape(rep, qb, qb) + bias_d[None]
    m0 = jnp.max(s_d, axis=-1, keepdims=True)
    e_d = jnp.exp2(s_d - m0)
    l0 = jnp.sum(e_d, axis=-1, keepdims=True)
    acc0 = _dot(e_d.reshape(rep * qb, qb), kvd).reshape(rep, qb, KV_PACK)

    q_aug = jnp.concatenate([q4, neg_sel4], axis=1)

    def chunk_step(c, carry):
        m, l, acc = carry
        k0 = pl.multiple_of(c * kc, kc)
        kv = kvs_ref[pl.ds(k0, kc), :]
        keys = jnp.concatenate([kv, blk_ref[pl.ds(k0, kc), :]], axis=1)
        s = _dot_nt(q_aug, keys).reshape(rep, qb, kc)
        m_new = jnp.maximum(m, jnp.max(s, axis=-1, keepdims=True))
        alpha = jnp.exp2(m - m_new)
        e = jnp.exp2(s - m_new)
        l = alpha * l + jnp.sum(e, axis=-1, keepdims=True)
        pv = _dot(e.reshape(rep * qb, kc), kv).reshape(rep, qb, KV_PACK)
        return m_new, l, alpha * acc + pv

    def pair_body(i, carry):
        return chunk_step(2 * i + 1, chunk_step(2 * i, carry))

    n_chunks = (t0 + SEL_BLOCK + kc - 1) // kc
    _, l_s, acc_s = lax.fori_loop(0, (n_chunks + 1) // 2, pair_body, (m0, l0, acc0))
    o_s = (acc_s * (1.0 / l_s)).reshape(rep * qb, KV_PACK)

    gates = jax.nn.sigmoid(gl_ref[...])
    outs = []
    for r in range(rep):
        rows = slice(r * qb, (r + 1) * qb)
        outs.append(gates[:, 3 * r:3 * r + 1] * o_c[rows] + gates[:, 3 * r + 1:3 * r + 2] * o_s[rows]
                    + gates[:, 3 * r + 2:3 * r + 3] * o_w[rows])
    for p in range(rep // 2):
        pair = jnp.where(lane < dh, pltpu.roll(outs[2 * p], dh, 1), outs[2 * p + 1])
        o_ref[:, p * KV_PACK:(p + 1) * KV_PACK] = pair.astype(o_ref.dtype)


def _nsa(q, gl, kv_cmp, kv6):
    b, s, _ = q.shape
    nb = s // Q_BLOCK
    gw = NSA_REP * NSA_HEAD_DIM
    n_cmp_rows = s // CMP_STRIDE
    n_slc = s // SEL_BLOCK
    assert (s // min(SEL_KEY_CHUNK, s)) % 2 == 0, "the chunk loop walks key chunks in pairs"
    key_block = jnp.arange(s, dtype=jnp.int32)[:, None] // SEL_BLOCK
    blk_onehot = (key_block == jnp.arange(n_slc, dtype=jnp.int32)[None, :]).astype(MXU_DTYPE)
    return pl.pallas_call(
        functools.partial(_nsa_kernel, seq=s),
        grid=(b, NSA_KV_HEADS, nb),
        in_specs=[
            pl.BlockSpec((None, Q_BLOCK, gw), lambda bb, g, i: (bb, i, g)),
            pl.BlockSpec((None, Q_BLOCK, GATE_PAD), lambda bb, g, i: (bb, i, g)),
            pl.BlockSpec((None, None, n_cmp_rows, KV_PACK), lambda bb, g, i: (bb, g, 0, 0)),
            pl.BlockSpec((None, None, s, KV_PACK), lambda bb, g, i: (bb, 2 + g, 0, 0)),
            pl.BlockSpec((None, None, s, KV_PACK), lambda bb, g, i: (bb, 4 + g, 0, 0)),
            pl.BlockSpec((s, n_slc), lambda bb, g, i: (0, 0)),
        ],
        out_specs=pl.BlockSpec((None, Q_BLOCK, gw), lambda bb, g, i: (bb, i, g)),
        out_shape=jax.ShapeDtypeStruct((b, s, NSA_WIDTH), MXU_DTYPE),
        compiler_params=_params("parallel", "parallel", "arbitrary"),
        name="nsa_attention",
    )(q, gl, kv_cmp, kv6, kv6, blk_onehot)


def _mix_out_kernel(x_ref, nsa_ref, zg_ref, zc_ref, zch_ref, gln_g_ref, gln_b_ref, ws_ref, bs_ref,
                    dww_ref, dwb_ref, cln_g_ref, cln_b_ref, og_ref, wo_ref, o_ref, hbuf_ref):
    tm = x_ref.shape[0]
    i = pl.program_id(1)

    def glu(z):
        z = z.astype(F32)
        return z[:, :CONV_WIDTH] * jax.nn.sigmoid(z[:, CONV_WIDTH:])

    halo = glu(zch_ref[...])
    hbuf_ref[0:CONV_HALO, :] = jnp.where(i == 0, 0.0, halo)
    hbuf_ref[CONV_HALO:CONV_HALO + tm, :] = glu(zc_ref[...])
    conv = jnp.broadcast_to(dwb_ref[...], (tm, CONV_WIDTH))
    for k in range(CONV_TAPS):
        off = CONV_HALO - (CONV_TAPS - 1) + k
        conv = conv + dww_ref[k:k + 1, :] * hbuf_ref[off:off + tm, :]
    y_conv = jax.nn.silu(_layer_norm(conv, cln_g_ref[...], cln_b_ref[...]))

    zg = _gelu_tanh(zg_ref[...].astype(F32))
    u = zg[:, :GMLP_WIDTH]
    v = _layer_norm(zg[:, GMLP_WIDTH:], gln_g_ref[...], gln_b_ref[...]).astype(MXU_DTYPE)
    row = lax.broadcasted_iota(jnp.int32, (GMLP_CHUNK, GMLP_CHUNK), 0)
    col = lax.broadcasted_iota(jnp.int32, (GMLP_CHUNK, GMLP_CHUNK), 1)
    grp = lax.broadcasted_iota(jnp.int32, (GMLP_CHUNK, GMLP_WIDTH), 1) // (GMLP_WIDTH // GMLP_GROUPS)
    w_causal = [jnp.where(row >= col, ws_ref[g], 0.0).astype(MXU_DTYPE) for g in range(GMLP_GROUPS)]
    bias = jnp.zeros((GMLP_CHUNK, GMLP_WIDTH), F32)
    for g in range(GMLP_GROUPS):
        bias = jnp.where(grp == g, bs_ref[:, g:g + 1], bias)
    mixed = []
    for c in range(tm // GMLP_CHUNK):
        vc = v[c * GMLP_CHUNK:(c + 1) * GMLP_CHUNK]
        acc = bias
        for g in range(GMLP_GROUPS):
            acc = acc + jnp.where(grp == g, jnp.dot(w_causal[g], vc, preferred_element_type=F32), 0.0)
        mixed.append(acc)
    y_gmlp = u * jnp.concatenate(mixed, axis=0)

    og = og_ref[...]
    n0, n1 = NSA_WIDTH, NSA_WIDTH + GMLP_WIDTH
    y = _dot(_rms(nsa_ref[...].astype(F32), og[:, :n0]), wo_ref[:n0, :])
    y = y + _dot(_rms(y_gmlp, og[:, n0:n1]), wo_ref[n0:n1, :])
    y = y + _dot(_rms(y_conv, og[:, n1:]), wo_ref[n1:, :])
    o_ref[...] = x_ref[...] + y


def _mix_out(x, y_nsa, zg, zc, p):
    b, s, d = x.shape
    tm = min(MIX_ROWS, s)
    halo_blocks = tm // CONV_HALO
    row2 = lambda a: a.reshape(1, -1)
    full = lambda shape: pl.BlockSpec(shape, lambda bb, i: (0,) * len(shape))
    tile = lambda w: pl.BlockSpec((None, tm, w), lambda bb, i: (bb, i, 0))
    return pl.pallas_call(
        _mix_out_kernel,
        grid=(b, s // tm),
        in_specs=[
            tile(d), tile(NSA_WIDTH), tile(2 * GMLP_WIDTH), tile(2 * CONV_WIDTH),
            pl.BlockSpec((None, CONV_HALO, 2 * CONV_WIDTH),
                         lambda bb, i: (bb, jnp.maximum(i * halo_blocks - 1, 0), 0)),
            full((1, GMLP_WIDTH)), full((1, GMLP_WIDTH)),
            full((GMLP_GROUPS, GMLP_CHUNK, GMLP_CHUNK)), full((GMLP_CHUNK, GMLP_GROUPS)),
            full((CONV_TAPS, CONV_WIDTH)), full((1, CONV_WIDTH)), full((1, CONV_WIDTH)), full((1, CONV_WIDTH)),
            full((1, d)), full((d, d)),
        ],
        out_specs=tile(d),
        out_shape=jax.ShapeDtypeStruct((b, s, d), F32),
        scratch_shapes=[pltpu.VMEM((CONV_HALO + tm, CONV_WIDTH), F32)],
        compiler_params=_params("parallel", "parallel"),
        name="mix_out",
    )(x, y_nsa, zg, zc, zc, row2(p["gmlp_ln_g"]), row2(p["gmlp_ln_b"]), p["gmlp_ws"], p["gmlp_bs"].T,
      p["conv_dw_w"], row2(p["conv_dw_b"]), row2(p["conv_ln_g"]), row2(p["conv_ln_b"]),
      row2(p["mix_out_g"]), p["w_out"].astype(MXU_DTYPE))


def _mem_kv_kernel(m_ref, g_ref, wk_ref, wv_ref, k_ref, v_ref):
    m = _rms(m_ref[...], g_ref[...]).astype(MXU_DTYPE)
    k_ref[...] = _dot(m, wk_ref[...]).astype(k_ref.dtype)
    v_ref[...] = _dot(m, wv_ref[...]).astype(v_ref.dtype)


def _mem_kv(mem, g, wk, wv):
    b, n, d = mem.shape
    blk = pl.BlockSpec((None, n, d), lambda bb: (bb, 0, 0))
    wspec = pl.BlockSpec((d, d), lambda bb: (0, 0))
    return pl.pallas_call(
        _mem_kv_kernel,
        grid=(b,),
        in_specs=[blk, pl.BlockSpec((1, d), lambda bb: (0, 0)), wspec, wspec],
        out_specs=[blk, blk],
        out_shape=[jax.ShapeDtypeStruct((b, n, d), MXU_DTYPE)] * 2,
        compiler_params=_params("parallel"),
        name="xattn_mem_kv",
    )(mem, g.reshape(1, d), wk.astype(MXU_DTYPE), wv.astype(MXU_DTYPE))


def _xattn_kernel(x_ref, g_ref, wq_ref, k_ref, v_ref, wo_ref, o_ref):
    dh = XATTN_HEAD_DIM
    x = x_ref[...]
    h = _rms(x, g_ref[...]).astype(MXU_DTYPE)
    y = x
    for hd in range(XATTN_HEADS):
        cols = slice(hd * dh, (hd + 1) * dh)
        q = _dot(h, wq_ref[:, cols]) * (dh ** -0.5)
        s = _dot_nt(q, k_ref[:, cols])
        e = jnp.exp(s - jnp.max(s, axis=-1, keepdims=True))
        p = e / jnp.sum(e, axis=-1, keepdims=True)
        y = y + _dot(_dot(p, v_ref[:, cols]), wo_ref[cols, :])
    o_ref[...] = y


def _xattn(x, g, wq, k, v, wo):
    b, s, d = x.shape
    n = k.shape[1]
    tm = min(XATTN_ROWS, s)
    tile = pl.BlockSpec((None, tm, d), lambda bb, i: (bb, i, 0))
    wspec = pl.BlockSpec((d, d), lambda bb, i: (0, 0))
    mspec = pl.BlockSpec((None, n, d), lambda bb, i: (bb, 0, 0))
    return pl.pallas_call(
        _xattn_kernel,
        grid=(b, s // tm),
        in_specs=[tile, pl.BlockSpec((1, d), lambda bb, i: (0, 0)), wspec, mspec, mspec, wspec],
        out_specs=tile,
        out_shape=jax.ShapeDtypeStruct((b, s, d), F32),
        compiler_params=_params("parallel", "parallel"),
        name="xattn",
    )(x, g.reshape(1, d), wq.astype(MXU_DTYPE), k, v, wo.astype(MXU_DTYPE))


def _ffn_kernel(x_ref, xh_ref, g_ref, wg_ref, wu_ref, cwg_ref, cwu_ref, cbg_ref, cbu_ref, wd_ref, fg_ref,
                o_ref, xn_ref, acc_ref, *, final_norm):
    tm = x_ref.shape[0]
    i, j = pl.program_id(1), pl.program_id(2)

    @pl.when(j == 0)
    def _():
        halo = jnp.where(i == 0, 0.0, xh_ref[...])
        xn_ref[0:FFN_HALO, :] = _rms(halo, g_ref[...]).astype(xn_ref.dtype)
        xn_ref[FFN_HALO:FFN_HALO + tm, :] = _rms(x_ref[...], g_ref[...]).astype(xn_ref.dtype)
        acc_ref[...] = jnp.zeros_like(acc_ref)

    xn = xn_ref[...]

    def conv(h, w_ref, b_ref):
        y = w_ref[2:3, :] * h + w_ref[1:2, :] * pltpu.roll(h, 1, 0) + w_ref[0:1, :] * pltpu.roll(h, 2, 0)
        return y[FFN_HALO:, :] + b_ref[...]

    gate = conv(_dot(xn, wg_ref[...]), cwg_ref, cbg_ref)
    up = conv(_dot(xn, wu_ref[...]), cwu_ref, cbu_ref)
    acc_ref[...] += _dot(jax.nn.silu(gate) * up, wd_ref[...])

    @pl.when(j == pl.num_programs(2) - 1)
    def _():
        y = x_ref[...] + acc_ref[...]
        o_ref[...] = _rms(y, fg_ref[...]) if final_norm else y


def _ffn(x, g, w_up, dw_w, dw_b, w_down, final_g, final_norm):
    b, s, d = x.shape
    tm = min(FFN_ROWS, s)
    tc = FFN_COLS
    nj = D_FF // tc
    halo_blocks = tm // FFN_HALO
    dw_b = dw_b.reshape(1, -1)
    tile = pl.BlockSpec((None, tm, d), lambda bb, i, j: (bb, i, 0))
    vec = pl.BlockSpec((1, d), lambda bb, i, j: (0, 0))
    return pl.pallas_call(
        functools.partial(_ffn_kernel, final_norm=final_norm),
        grid=(b, s // tm, nj),
        in_specs=[
            tile,
            pl.BlockSpec((None, FFN_HALO, d), lambda bb, i, j: (bb, jnp.maximum(i * halo_blocks - 1, 0), 0)),
            vec,
            pl.BlockSpec((d, tc), lambda bb, i, j: (0, j)),
            pl.BlockSpec((d, tc), lambda bb, i, j: (0, nj + j)),
            pl.BlockSpec((FFN_CONV_TAPS, tc), lambda bb, i, j: (0, j)),
            pl.BlockSpec((FFN_CONV_TAPS, tc), lambda bb, i, j: (0, nj + j)),
            pl.BlockSpec((1, tc), lambda bb, i, j: (0, j)),
            pl.BlockSpec((1, tc), lambda bb, i, j: (0, nj + j)),
            pl.BlockSpec((tc, d), lambda bb, i, j: (j, 0)),
            vec,
        ],
        out_specs=tile,
        out_shape=jax.ShapeDtypeStruct((b, s, d), F32),
        scratch_shapes=[pltpu.VMEM((FFN_HALO + tm, d), MXU_DTYPE), pltpu.VMEM((tm, d), F32)],
        compiler_params=_params("parallel", "parallel", "arbitrary"),
        name="conv_ffn",
    )(x, x, g.reshape(1, d), w_up.astype(MXU_DTYPE), w_up.astype(MXU_DTYPE), dw_w, dw_w, dw_b, dw_b,
      w_down.astype(MXU_DTYPE), final_g.reshape(1, d))


def kernel(x, mem, mix_norm_g, w_in, cmp_pe, cmp_w1, cmp_w2, gmlp_ln_g, gmlp_ln_b, gmlp_ws, gmlp_bs, conv_dw_w, conv_dw_b, conv_ln_g, conv_ln_b, mix_out_g, w_out, xattn_norm_g, mem_norm_g, xattn_wq, xattn_wk, xattn_wv, xattn_wo, ffn_norm_g, ffn_w_up, ffn_dw_w, ffn_dw_b, ffn_w_down, final_norm_g):
    depth = w_in.shape[0]
    for l in range(depth):
        q, kv6, zg, zc, gl = _in_proj(x, mix_norm_g[l], _permute_w_in(w_in[l]))
        kv_cmp = _compress(kv6, *_pack_cmp_weights(cmp_pe[l], cmp_w1[l], cmp_w2[l]))
        y_nsa = _nsa(q, gl, kv_cmp, kv6)
        mix = dict(gmlp_ln_g=gmlp_ln_g[l], gmlp_ln_b=gmlp_ln_b[l], gmlp_ws=gmlp_ws[l], gmlp_bs=gmlp_bs[l],
                   conv_dw_w=conv_dw_w[l], conv_dw_b=conv_dw_b[l], conv_ln_g=conv_ln_g[l],
                   conv_ln_b=conv_ln_b[l], mix_out_g=mix_out_g[l], w_out=w_out[l])
        x = _mix_out(x, y_nsa, zg, zc, mix)
        mk, mv = _mem_kv(mem, mem_norm_g[l], xattn_wk[l], xattn_wv[l])
        x = _xattn(x, xattn_norm_g[l], xattn_wq[l], mk, mv, xattn_wo[l])
        x = _ffn(x, ffn_norm_g[l], ffn_w_up[l], ffn_dw_w[l], ffn_dw_b[l], ffn_w_down[l], final_norm_g,
                 final_norm=(l == depth - 1))
    return x
```

```python
import functools

import jax
import jax.numpy as jnp
from jax import lax
from jax.experimental import pallas as pl
from jax.experimental.pallas import tpu as pltpu

D_MODEL = 1024
NSA_HEADS = 8
NSA_KV_HEADS = 2
NSA_HEAD_DIM = 64
NSA_REP = NSA_HEADS // NSA_KV_HEADS
CMP_BLOCK = 32
CMP_STRIDE = 16
SEL_BLOCK = 64
N_SELECT = 16
WINDOW = 512
Q_BLOCK = 128
NSA_WIDTH = NSA_HEADS * NSA_HEAD_DIM
KV_WIDTH = NSA_KV_HEADS * NSA_HEAD_DIM
GMLP_WIDTH = 256
GMLP_GROUPS = 4
GMLP_CHUNK = 128
CONV_WIDTH = 256
CONV_TAPS = 31
XATTN_HEADS = 4
XATTN_HEAD_DIM = D_MODEL // XATTN_HEADS
D_FF = 2816
FFN_CONV_TAPS = 3
EPS = 1e-6
NEG_INF = -1e30
TINY = 1e-30
FORCE = 1e4
LOG2E = 1.4426950408889634
MAX_FLOOR = -1e29
KNOCKOUT = -3.0e38

LANES = 128
BF16_SUBLANES = 16
VMEM_LIMIT_BYTES = 48 * 1024 * 1024

MXU_DTYPE = jnp.bfloat16
F32 = jnp.float32

IN_PROJ_ROWS = 512
NSA_Q_TILE = 128
SEL_KEY_CHUNK = 512
MIX_ROWS = 512
CONV_HALO = 32
XATTN_ROWS = 512
FFN_ROWS = 512
FFN_COLS = 256
FFN_HALO = 16
GATE_PAD = LANES
KV_PACK = 2 * NSA_HEAD_DIM


def _params(*semantics):
    return pltpu.CompilerParams(dimension_semantics=semantics, vmem_limit_bytes=VMEM_LIMIT_BYTES)


def _dot(a, b):
    return jnp.dot(a.astype(MXU_DTYPE), b.astype(MXU_DTYPE), preferred_element_type=F32)


def _dot_nt(a, b):
    return lax.dot_general(a.astype(MXU_DTYPE), b.astype(MXU_DTYPE), (((1,), (1,)), ((), ())),
                           preferred_element_type=F32)


def _rms(x, g):
    return x * lax.rsqrt(jnp.mean(x * x, axis=-1, keepdims=True) + EPS) * g


def _layer_norm(x, g, b):
    mu = jnp.mean(x, axis=-1, keepdims=True)
    xc = x - mu
    var = jnp.mean(xc * xc, axis=-1, keepdims=True)
    return xc * lax.rsqrt(var + EPS) * g + b


def _gelu_tanh(x):
    return 0.5 * x * (1.0 + jnp.tanh(0.7978845608028654 * (x + 0.044715 * (x * x * x))))


def _masked_softmax(s, valid):
    s = jnp.where(valid, s, NEG_INF)
    m = jnp.max(s, axis=-1, keepdims=True)
    e = jnp.where(valid, jnp.exp(s - m), 0.0)
    return e / jnp.maximum(jnp.sum(e, axis=-1, keepdims=True), TINY)


IN_Q = (0, NSA_WIDTH)
IN_KV = (IN_Q[1], IN_Q[1] + 6 * KV_PACK)
IN_ZG = (IN_KV[1], IN_KV[1] + 2 * GMLP_WIDTH)
IN_ZC = (IN_ZG[1], IN_ZG[1] + 2 * CONV_WIDTH)
IN_GL = (IN_ZC[1], IN_ZC[1] + NSA_KV_HEADS * GATE_PAD)
IN_COLS = IN_GL[1]


def _permute_w_in(w):
    off_kv = NSA_WIDTH
    off_gl = NSA_WIDTH + 6 * KV_WIDTH
    off_zg = off_gl + NSA_HEADS * 3
    off_zc = off_zg + 2 * GMLP_WIDTH
    parts = [w[:, :NSA_WIDTH]]
    for branch in range(3):
        k0 = off_kv + (2 * branch) * KV_WIDTH
        v0 = off_kv + (2 * branch + 1) * KV_WIDTH
        for g in range(NSA_KV_HEADS):
            parts.append(w[:, k0 + g * NSA_HEAD_DIM:k0 + (g + 1) * NSA_HEAD_DIM])
            parts.append(w[:, v0 + g * NSA_HEAD_DIM:v0 + (g + 1) * NSA_HEAD_DIM])
    parts.append(w[:, off_zg:off_zg + 2 * GMLP_WIDTH])
    parts.append(w[:, off_zc:off_zc + 2 * CONV_WIDTH])
    n_gate = NSA_REP * 3
    for g in range(NSA_KV_HEADS):
        parts.append(jnp.pad(w[:, off_gl + g * n_gate:off_gl + (g + 1) * n_gate], ((0, 0), (0, GATE_PAD - n_gate))))
    return jnp.concatenate(parts, axis=1).astype(MXU_DTYPE)


def _in_proj_kernel(x_ref, g_ref, w_ref, q_ref, kv_ref, zg_ref, zc_ref, gl_ref):
    h = _rms(x_ref[...], g_ref[...]).astype(MXU_DTYPE)
    q_ref[...] = _dot(h, w_ref[:, IN_Q[0]:IN_Q[1]]).astype(q_ref.dtype)
    for j in range(6):
        c0 = IN_KV[0] + j * KV_PACK
        kv_ref[j] = _dot(h, w_ref[:, c0:c0 + KV_PACK]).astype(kv_ref.dtype)
    zg_ref[...] = _dot(h, w_ref[:, IN_ZG[0]:IN_ZG[1]]).astype(zg_ref.dtype)
    zc_ref[...] = _dot(h, w_ref[:, IN_ZC[0]:IN_ZC[1]]).astype(zc_ref.dtype)
    gl_ref[...] = _dot(h, w_ref[:, IN_GL[0]:IN_GL[1]])


def _in_proj(x, g, w):
    b, s, d = x.shape
    tm = IN_PROJ_ROWS
    return pl.pallas_call(
        _in_proj_kernel,
        grid=(b, s // tm),
        in_specs=[
            pl.BlockSpec((None, tm, d), lambda bb, i: (bb, i, 0)),
            pl.BlockSpec((1, d), lambda bb, i: (0, 0)),
            pl.BlockSpec((d, IN_COLS), lambda bb, i: (0, 0)),
        ],
        out_specs=[
            pl.BlockSpec((None, tm, NSA_WIDTH), lambda bb, i: (bb, i, 0)),
            pl.BlockSpec((None, 6, tm, KV_PACK), lambda bb, i: (bb, 0, i, 0)),
            pl.BlockSpec((None, tm, 2 * GMLP_WIDTH), lambda bb, i: (bb, i, 0)),
            pl.BlockSpec((None, tm, 2 * CONV_WIDTH), lambda bb, i: (bb, i, 0)),
            pl.BlockSpec((None, tm, NSA_KV_HEADS * GATE_PAD), lambda bb, i: (bb, i, 0)),
        ],
        out_shape=[
            jax.ShapeDtypeStruct((b, s, NSA_WIDTH), MXU_DTYPE),
            jax.ShapeDtypeStruct((b, 6, s, KV_PACK), MXU_DTYPE),
            jax.ShapeDtypeStruct((b, s, 2 * GMLP_WIDTH), MXU_DTYPE),
            jax.ShapeDtypeStruct((b, s, 2 * CONV_WIDTH), MXU_DTYPE),
            jax.ShapeDtypeStruct((b, s, NSA_KV_HEADS * GATE_PAD), F32),
        ],
        compiler_params=_params("parallel", "parallel"),
        name="in_proj",
    )(x, g.reshape(1, d), w)


CMP_ROW = CMP_STRIDE * KV_PACK


def _pack_cmp_weights(pe, w1, w2):
    dh = NSA_HEAD_DIM
    w1r = w1.reshape(2, CMP_BLOCK, dh, dh)
    z = jnp.zeros_like(w1r[0])
    wk = jnp.concatenate([w1r[0], z], axis=-1)
    wv = jnp.concatenate([z, w1r[1]], axis=-1)
    w1p = jnp.concatenate([wk, wv], axis=1).reshape(CMP_BLOCK * KV_PACK, KV_PACK)
    z2 = jnp.zeros_like(w2[0])
    w2p = jnp.concatenate([jnp.concatenate([w2[0], z2], axis=1), jnp.concatenate([z2, w2[1]], axis=1)], axis=0)
    pep = jnp.concatenate([pe[0], pe[1]], axis=-1).reshape(1, CMP_BLOCK * KV_PACK)
    pep = jnp.broadcast_to(pep, (8, CMP_BLOCK * KV_PACK))
    return pep.astype(MXU_DTYPE), w1p.astype(MXU_DTYPE), w2p.astype(MXU_DTYPE)


def _compress_kernel(c_ref, pe_ref, w1_ref, w2_ref, o_ref):
    c = c_ref[...]
    n = c.shape[0]
    first = _dot(c, w1_ref[:CMP_ROW, :])
    second = _dot(c, w1_ref[CMP_ROW:, :])
    pe_term = _dot(pe_ref[...], w1_ref[...])[0:1, :]
    h = first + pltpu.roll(second, n - 1, 0) + pe_term
    o_ref[...] = _dot(_gelu_tanh(h), w2_ref[...]).astype(o_ref.dtype)


def _compress(kv6, pep, w1p, w2p):
    b, _, s, _ = kv6.shape
    n = s // CMP_STRIDE
    c = kv6.reshape(b, 6, n, CMP_ROW)
    return pl.pallas_call(
        _compress_kernel,
        grid=(b, NSA_KV_HEADS),
        in_specs=[
            pl.BlockSpec((None, None, n, CMP_ROW), lambda bb, g: (bb, g, 0, 0)),
            pl.BlockSpec((8, 2 * CMP_ROW), lambda bb, g: (0, 0)),
            pl.BlockSpec((2 * CMP_ROW, KV_PACK), lambda bb, g: (0, 0)),
            pl.BlockSpec((KV_PACK, KV_PACK), lambda bb, g: (0, 0)),
        ],
        out_specs=pl.BlockSpec((None, None, n, KV_PACK), lambda bb, g: (bb, g, 0, 0)),
        out_shape=jax.ShapeDtypeStruct((b, NSA_KV_HEADS, n, KV_PACK), MXU_DTYPE),
        compiler_params=_params("parallel", "parallel"),
        name="nsa_compress",
    )(c, pep, w1p, w2p)


def _split3(x):
    a = x.astype(MXU_DTYPE)
    r = x - a.astype(F32)
    b = r.astype(MXU_DTYPE)
    c = (r - b.astype(F32)).astype(MXU_DTYPE)
    return a, b, c


def _softmax2_parts(s):
    m = jnp.maximum(jnp.max(s, axis=-1, keepdims=True), MAX_FLOOR)
    e = jnp.exp2(s - m)
    return e, 1.0 / jnp.maximum(jnp.sum(e, axis=-1, keepdims=True), TINY)


def _nsa_kernel(q_ref, gl_ref, cmp_ref, kvs_ref, kvw_ref, blk_ref, ovl_ref, diag_ref, o_ref, s_buf, e_buf, ov_ref,
                *, seq):
    dh, qb, rep = NSA_HEAD_DIM, q_ref.shape[0], NSA_REP
    n_cmp_rows = seq // CMP_STRIDE
    n_slc = seq // SEL_BLOCK
    k_sel = min(N_SELECT, n_slc)
    kc = min(SEL_KEY_CHUNK, seq)
    win_keys = WINDOW + qb
    sel_shift = SEL_BLOCK.bit_length() - 1

    bi = pl.program_id(2)
    t0 = bi * qb

    @pl.when(bi == 0)
    def _():
        all_lanes = lax.broadcasted_iota(jnp.int32, ov_ref.shape, 1)
        ov_ref[...] = jnp.where(all_lanes < dh, 1.0, kvs_ref[...].astype(F32)).astype(ov_ref.dtype)

    t_col = t0 + lax.broadcasted_iota(jnp.int32, (qb, 1), 0)
    t_row = t0 + lax.broadcasted_iota(jnp.int32, (1, qb), 1)

    lane = lax.broadcasted_iota(jnp.int32, (qb, KV_PACK), 1)
    heads = []
    for p in range(rep // 2):
        pair = q_ref[:, p * KV_PACK:(p + 1) * KV_PACK].astype(F32) * (dh ** -0.5 * LOG2E)
        heads.append(jnp.where(lane < dh, pair, 0.0))
        heads.append(jnp.where(lane < dh, pltpu.roll(pair, dh, 1), 0.0))
    q4 = jnp.concatenate(heads, axis=0).astype(MXU_DTYPE)

    kvc = cmp_ref[...]
    n_idx = lax.broadcasted_iota(jnp.int32, (1, n_cmp_rows), 1)
    valid_c = (n_idx * CMP_STRIDE + (CMP_BLOCK - 1) <= t_col) & (n_idx < n_cmp_rows - 1)
    bias_c = jnp.where(valid_c, 0.0, NEG_INF)
    e_c, inv_c = _softmax2_parts(_dot_nt(q4, kvc).reshape(rep, qb, n_cmp_rows) + bias_c[None])
    p_c = e_c * inv_c
    o_c = _dot(p_c.reshape(rep * qb, n_cmp_rows), kvc)

    p_sum = p_c[0]
    for r in range(1, rep):
        p_sum = p_sum + p_c[r]
    imp = None
    for term in _split3(p_sum):
        d = _dot_nt(ovl_ref[...], term)
        imp = d if imp is None else imp + d
    jj = lax.broadcasted_iota(jnp.int32, (n_slc, qb), 0)
    cur = t_row >> sel_shift
    forced = (jj == 0) | (jj == cur) | (jj == cur - 1)
    imp = jnp.where(forced, FORCE, imp)
    imp = jnp.where(jj <= cur, imp, -FORCE)
    jf = jj.astype(F32)
    sel = jnp.zeros((n_slc, qb), F32)
    for _ in range(k_sel):
        mx = jnp.max(imp, axis=0, keepdims=True)
        first = jnp.min(jnp.where(imp == mx, jf, float(n_slc)), axis=0, keepdims=True)
        pick = jf == first
        sel = jnp.where(pick, 1.0, sel)
        imp = jnp.where(pick, KNOCKOUT, imp)
    bias_t = jnp.where((sel > 0.5) & (jj < cur), 0.0, NEG_INF)
    neg_sel = bias_t.T.astype(MXU_DTYPE)
    neg_sel4 = jnp.concatenate([neg_sel] * rep, axis=0)

    w0 = pl.multiple_of(jnp.maximum(t0 - WINDOW, 0), qb)
    kvw = kvw_ref[pl.ds(w0, win_keys), :]
    kp = w0 + lax.broadcasted_iota(jnp.int32, (1, win_keys), 1)
    bias_w = jnp.where((kp <= t_col) & (kp > t_col - WINDOW), 0.0, NEG_INF)
    e_w, inv_w = _softmax2_parts(_dot_nt(q4, kvw).reshape(rep, qb, win_keys) + bias_w[None])
    o_w = _dot(e_w.reshape(rep * qb, win_keys), kvw) * inv_w.reshape(rep * qb, 1)

    own = pl.ds(pl.multiple_of(t0, qb), qb)
    s_d = _dot_nt(q4, kvs_ref[own, :]).reshape(rep, qb, qb) + diag_ref[...][None]
    m0 = jnp.max(s_d, axis=-1, keepdims=True)
    e_d = jnp.exp2(s_d - m0)
    acc0 = _dot(e_d.reshape(rep * qb, qb), ov_ref[own, :]).reshape(rep, qb, KV_PACK)

    q_aug = jnp.concatenate([q4, neg_sel4], axis=1)

    def scores(c, slot):
        k0 = pl.multiple_of(c * kc, kc)
        keys = jnp.concatenate([kvs_ref[pl.ds(k0, kc), :], blk_ref[pl.ds(k0, kc), :]], axis=1)
        s_buf[slot] = _dot_nt(q_aug, keys)

    def softmax_step(slot, m):
        s = s_buf[slot].reshape(rep, qb, kc)
        m_new = jnp.maximum(m, jnp.max(s, axis=-1, keepdims=True))
        e_buf[slot] = jnp.exp2(s - m_new).reshape(rep * qb, kc).astype(e_buf.dtype)
        return m_new, jnp.exp2(m - m_new)

    def weighted_values(c, slot, alpha, acc):
        ov = ov_ref[pl.ds(pl.multiple_of(c * kc, kc), kc), :]
        return alpha * acc + _dot(e_buf[slot], ov).reshape(rep, qb, KV_PACK)

    last_chunk = seq // kc - 1

    def pair_body(i, carry):
        m, acc, alpha_prev = carry
        c = 2 * i
        scores(c + 1, 1)
        m, alpha_even = softmax_step(0, m)
        acc = weighted_values(jnp.maximum(c - 1, 0), 1, alpha_prev, acc)
        scores(jnp.minimum(c + 2, last_chunk), 0)
        m, alpha_odd = softmax_step(1, m)
        acc = weighted_values(c, 0, alpha_even, acc)
        return m, acc, alpha_odd

    n_pairs = ((t0 + qb - SEL_BLOCK + kc - 1) // kc + 1) // 2
    scores(0, 0)
    e_buf[1] = jnp.zeros(e_buf.shape[1:], e_buf.dtype)
    _, acc_s, alpha_last = lax.fori_loop(0, n_pairs, pair_body, (m0, acc0, jnp.ones_like(m0)))
    acc_s = weighted_values(2 * n_pairs - 1, 1, alpha_last, acc_s).reshape(rep * qb, KV_PACK)
    lane4 = lax.broadcasted_iota(jnp.int32, (rep * qb, KV_PACK), 1)
    o_s = acc_s / jnp.where(lane4 < dh, 1.0, pltpu.roll(acc_s, dh, 1))

    gates = jax.nn.sigmoid(gl_ref[...])
    outs = []
    for r in range(rep):
        rows = slice(r * qb, (r + 1) * qb)
        outs.append(gates[:, 3 * r:3 * r + 1] * o_c[rows] + gates[:, 3 * r + 1:3 * r + 2] * o_s[rows]
                    + gates[:, 3 * r + 2:3 * r + 3] * o_w[rows])
    for p in range(rep // 2):
        pair = jnp.where(lane < dh, pltpu.roll(outs[2 * p], dh, 1), outs[2 * p + 1])
        o_ref[:, p * KV_PACK:(p + 1) * KV_PACK] = pair.astype(o_ref.dtype)


def _nsa(q, gl, kv_cmp, kv6):
    b, s, _ = q.shape
    qt = min(NSA_Q_TILE, s)
    nb = s // qt
    gw = NSA_REP * NSA_HEAD_DIM
    n_cmp_rows = s // CMP_STRIDE
    n_slc = s // SEL_BLOCK
    kc = min(SEL_KEY_CHUNK, s)
    assert (s // kc) % 2 == 0, "the chunk loop walks key chunks in pairs"
    key_block = jnp.arange(s, dtype=jnp.int32)[:, None] // SEL_BLOCK
    blk_onehot = (key_block == jnp.arange(n_slc, dtype=jnp.int32)[None, :]).astype(MXU_DTYPE)
    sj = jnp.arange(n_slc, dtype=jnp.int32)[:, None] * SEL_BLOCK
    ci = jnp.arange(n_cmp_rows, dtype=jnp.int32)[None, :] * CMP_STRIDE
    overlap_t = ((ci < sj + SEL_BLOCK) & (ci + CMP_BLOCK > sj)).astype(MXU_DTYPE)
    ti = jnp.arange(qt, dtype=jnp.int32)
    own_block = (ti[:, None] // SEL_BLOCK == ti[None, :] // SEL_BLOCK) & (ti[None, :] <= ti[:, None])
    diag_bias = jnp.where(own_block, 0.0, NEG_INF).astype(F32)
    const = lambda shape: pl.BlockSpec(shape, lambda bb, g, i: (0,) * len(shape))
    return pl.pallas_call(
        functools.partial(_nsa_kernel, seq=s),
        grid=(b, NSA_KV_HEADS, nb),
        in_specs=[
            pl.BlockSpec((None, qt, gw), lambda bb, g, i: (bb, i, g)),
            pl.BlockSpec((None, qt, GATE_PAD), lambda bb, g, i: (bb, i, g)),
            pl.BlockSpec((None, None, n_cmp_rows, KV_PACK), lambda bb, g, i: (bb, g, 0, 0)),
            pl.BlockSpec((None, None, s, KV_PACK), lambda bb, g, i: (bb, 2 + g, 0, 0)),
            pl.BlockSpec((None, None, s, KV_PACK), lambda bb, g, i: (bb, 4 + g, 0, 0)),
            const((s, n_slc)), const((n_slc, n_cmp_rows)), const((qt, qt)),
        ],
        out_specs=pl.BlockSpec((None, qt, gw), lambda bb, g, i: (bb, i, g)),
        out_shape=jax.ShapeDtypeStruct((b, s, NSA_WIDTH), MXU_DTYPE),
        scratch_shapes=[pltpu.VMEM((2, NSA_REP * qt, kc), F32),
                        pltpu.VMEM((2, NSA_REP * qt, kc), MXU_DTYPE),
                        pltpu.VMEM((s, KV_PACK), MXU_DTYPE)],
        compiler_params=_params("parallel", "parallel", "arbitrary"),
        name="nsa_attention",
    )(q, gl, kv_cmp, kv6, kv6, blk_onehot, overlap_t, diag_bias)


def _mix_out_kernel(x_ref, nsa_ref, zg_ref, zc_ref, zch_ref, gln_g_ref, gln_b_ref, ws_ref, bs_ref,
                    dww_ref, dwb_ref, cln_g_ref, cln_b_ref, og_ref, wo_ref, o_ref, hbuf_ref):
    tm = x_ref.shape[0]
    i = pl.program_id(1)

    def glu(z):
        z = z.astype(F32)
        return z[:, :CONV_WIDTH] * jax.nn.sigmoid(z[:, CONV_WIDTH:])

    halo = glu(zch_ref[...])
    hbuf_ref[0:CONV_HALO, :] = jnp.where(i == 0, 0.0, halo)
    hbuf_ref[CONV_HALO:CONV_HALO + tm, :] = glu(zc_ref[...])
    conv = jnp.broadcast_to(dwb_ref[...], (tm, CONV_WIDTH))
    for k in range(CONV_TAPS):
        off = CONV_HALO - (CONV_TAPS - 1) + k
        conv = conv + dww_ref[k:k + 1, :] * hbuf_ref[off:off + tm, :]
    y_conv = jax.nn.silu(_layer_norm(conv, cln_g_ref[...], cln_b_ref[...]))

    zg = _gelu_tanh(zg_ref[...].astype(F32))
    u = zg[:, :GMLP_WIDTH]
    v = _layer_norm(zg[:, GMLP_WIDTH:], gln_g_ref[...], gln_b_ref[...]).astype(MXU_DTYPE)
    row = lax.broadcasted_iota(jnp.int32, (GMLP_CHUNK, GMLP_CHUNK), 0)
    col = lax.broadcasted_iota(jnp.int32, (GMLP_CHUNK, GMLP_CHUNK), 1)
    grp = lax.broadcasted_iota(jnp.int32, (GMLP_CHUNK, GMLP_WIDTH), 1) // (GMLP_WIDTH // GMLP_GROUPS)
    w_causal = [jnp.where(row >= col, ws_ref[g], 0.0).astype(MXU_DTYPE) for g in range(GMLP_GROUPS)]
    bias = jnp.zeros((GMLP_CHUNK, GMLP_WIDTH), F32)
    for g in range(GMLP_GROUPS):
        bias = jnp.where(grp == g, bs_ref[:, g:g + 1], bias)
    mixed = []
    for c in range(tm // GMLP_CHUNK):
        vc = v[c * GMLP_CHUNK:(c + 1) * GMLP_CHUNK]
        acc = bias
        for g in range(GMLP_GROUPS):
            acc = acc + jnp.where(grp == g, jnp.dot(w_causal[g], vc, preferred_element_type=F32), 0.0)
        mixed.append(acc)
    y_gmlp = u * jnp.concatenate(mixed, axis=0)

    og = og_ref[...]
    n0, n1 = NSA_WIDTH, NSA_WIDTH + GMLP_WIDTH
    y = _dot(_rms(nsa_ref[...].astype(F32), og[:, :n0]), wo_ref[:n0, :])
    y = y + _dot(_rms(y_gmlp, og[:, n0:n1]), wo_ref[n0:n1, :])
    y = y + _dot(_rms(y_conv, og[:, n1:]), wo_ref[n1:, :])
    o_ref[...] = x_ref[...] + y


def _mix_out(x, y_nsa, zg, zc, p):
    b, s, d = x.shape
    tm = min(MIX_ROWS, s)
    halo_blocks = tm // CONV_HALO
    row2 = lambda a: a.reshape(1, -1)
    full = lambda shape: pl.BlockSpec(shape, lambda bb, i: (0,) * len(shape))
    tile = lambda w: pl.BlockSpec((None, tm, w), lambda bb, i: (bb, i, 0))
    return pl.pallas_call(
        _mix_out_kernel,
        grid=(b, s // tm),
        in_specs=[
            tile(d), tile(NSA_WIDTH), tile(2 * GMLP_WIDTH), tile(2 * CONV_WIDTH),
            pl.BlockSpec((None, CONV_HALO, 2 * CONV_WIDTH),
                         lambda bb, i: (bb, jnp.maximum(i * halo_blocks - 1, 0), 0)),
            full((1, GMLP_WIDTH)), full((1, GMLP_WIDTH)),
            full((GMLP_GROUPS, GMLP_CHUNK, GMLP_CHUNK)), full((GMLP_CHUNK, GMLP_GROUPS)),
            full((CONV_TAPS, CONV_WIDTH)), full((1, CONV_WIDTH)), full((1, CONV_WIDTH)), full((1, CONV_WIDTH)),
            full((1, d)), full((d, d)),
        ],
        out_specs=tile(d),
        out_shape=jax.ShapeDtypeStruct((b, s, d), F32),
        scratch_shapes=[pltpu.VMEM((CONV_HALO + tm, CONV_WIDTH), F32)],
        compiler_params=_params("parallel", "parallel"),
        name="mix_out",
    )(x, y_nsa, zg, zc, zc, row2(p["gmlp_ln_g"]), row2(p["gmlp_ln_b"]), p["gmlp_ws"], p["gmlp_bs"].T,
      p["conv_dw_w"], row2(p["conv_dw_b"]), row2(p["conv_ln_g"]), row2(p["conv_ln_b"]),
      row2(p["mix_out_g"]), p["w_out"].astype(MXU_DTYPE))


def _mem_kv_kernel(m_ref, g_ref, wk_ref, wv_ref, k_ref, v_ref):
    m = _rms(m_ref[...], g_ref[...]).astype(MXU_DTYPE)
    k_ref[...] = _dot(m, wk_ref[...]).astype(k_ref.dtype)
    v_ref[...] = _dot(m, wv_ref[...]).astype(v_ref.dtype)


def _mem_kv(mem, g, wk, wv):
    b, n, d = mem.shape
    blk = pl.BlockSpec((None, n, d), lambda bb: (bb, 0, 0))
    wspec = pl.BlockSpec((d, d), lambda bb: (0, 0))
    return pl.pallas_call(
        _mem_kv_kernel,
        grid=(b,),
        in_specs=[blk, pl.BlockSpec((1, d), lambda bb: (0, 0)), wspec, wspec],
        out_specs=[blk, blk],
        out_shape=[jax.ShapeDtypeStruct((b, n, d), MXU_DTYPE)] * 2,
        compiler_params=_params("parallel"),
        name="xattn_mem_kv",
    )(mem, g.reshape(1, d), wk.astype(MXU_DTYPE), wv.astype(MXU_DTYPE))


def _xattn_kernel(x_ref, g_ref, wq_ref, k_ref, v_ref, wo_ref, o_ref):
    dh = XATTN_HEAD_DIM
    x = x_ref[...]
    h = _rms(x, g_ref[...]).astype(MXU_DTYPE)
    y = x
    for hd in range(XATTN_HEADS):
        cols = slice(hd * dh, (hd + 1) * dh)
        q = _dot(h, wq_ref[:, cols]) * (dh ** -0.5)
        s = _dot_nt(q, k_ref[:, cols])
        e = jnp.exp(s - jnp.max(s, axis=-1, keepdims=True))
        p = e / jnp.sum(e, axis=-1, keepdims=True)
        y = y + _dot(_dot(p, v_ref[:, cols]), wo_ref[cols, :])
    o_ref[...] = y


def _xattn(x, g, wq, k, v, wo):
    b, s, d = x.shape
    n = k.shape[1]
    tm = min(XATTN_ROWS, s)
    tile = pl.BlockSpec((None, tm, d), lambda bb, i: (bb, i, 0))
    wspec = pl.BlockSpec((d, d), lambda bb, i: (0, 0))
    mspec = pl.BlockSpec((None, n, d), lambda bb, i: (bb, 0, 0))
    return pl.pallas_call(
        _xattn_kernel,
        grid=(b, s // tm),
        in_specs=[tile, pl.BlockSpec((1, d), lambda bb, i: (0, 0)), wspec, mspec, mspec, wspec],
        out_specs=tile,
        out_shape=jax.ShapeDtypeStruct((b, s, d), F32),
        compiler_params=_params("parallel", "parallel"),
        name="xattn",
    )(x, g.reshape(1, d), wq.astype(MXU_DTYPE), k, v, wo.astype(MXU_DTYPE))


def _ffn_kernel(x_ref, xh_ref, g_ref, wup_ref, cw_ref, cb_ref, wd_ref, fg_ref, o_ref, xn_ref, h_buf, act_buf,
                *, final_norm):
    tm = x_ref.shape[0]
    tc = FFN_COLS
    i = pl.program_id(1)
    halo = jnp.where(i == 0, 0.0, xh_ref[...])
    xn_ref[0:FFN_HALO, :] = _rms(halo, g_ref[...]).astype(xn_ref.dtype)
    xn_ref[FFN_HALO:FFN_HALO + tm, :] = _rms(x_ref[...], g_ref[...]).astype(xn_ref.dtype)
    xn = xn_ref[...]

    def conv(slot, half, c0):
        cols = slice(c0, c0 + tc)
        y = cb_ref[:, cols]
        for k in range(FFN_CONV_TAPS):
            shift = FFN_CONV_TAPS - 1 - k
            y = y + cw_ref[k:k + 1, cols] * h_buf[slot, half, FFN_HALO - shift:FFN_HALO - shift + tm, :]
        return y

    for j in range(D_FF // tc):
        slot = j % 2
        gate_c0, up_c0 = j * tc, D_FF + j * tc
        h_buf[slot, 0] = _dot(xn, wup_ref[:, gate_c0:gate_c0 + tc])
        h_buf[slot, 1] = _dot(xn, wup_ref[:, up_c0:up_c0 + tc])
        act = jax.nn.silu(conv(slot, 0, gate_c0)) * conv(slot, 1, up_c0)
        act_buf[:, j * tc:(j + 1) * tc] = act.astype(act_buf.dtype)

    y = x_ref[...] + _dot(act_buf[...], wd_ref[...])
    o_ref[...] = _rms(y, fg_ref[...]) if final_norm else y


def _ffn(x, g, w_up, dw_w, dw_b, w_down, final_g, final_norm):
    b, s, d = x.shape
    tm = min(FFN_ROWS, s)
    halo_blocks = tm // FFN_HALO
    tile = pl.BlockSpec((None, tm, d), lambda bb, i: (bb, i, 0))
    resident = lambda shape: pl.BlockSpec(shape, lambda bb, i: (0,) * len(shape), pipeline_mode=pl.Buffered(1))
    return pl.pallas_call(
        functools.partial(_ffn_kernel, final_norm=final_norm),
        grid=(b, s // tm),
        in_specs=[
            tile,
            pl.BlockSpec((None, FFN_HALO, d), lambda bb, i: (bb, jnp.maximum(i * halo_blocks - 1, 0), 0)),
            resident((1, d)),
            resident((d, 2 * D_FF)),
            resident((FFN_CONV_TAPS, 2 * D_FF)),
            resident((1, 2 * D_FF)),
            resident((D_FF, d)),
            resident((1, d)),
        ],
        out_specs=tile,
        out_shape=jax.ShapeDtypeStruct((b, s, d), F32),
        scratch_shapes=[pltpu.VMEM((FFN_HALO + tm, d), MXU_DTYPE),
                        pltpu.VMEM((2, 2, FFN_HALO + tm, FFN_COLS), F32),
                        pltpu.VMEM((tm, D_FF), MXU_DTYPE)],
        compiler_params=_params("parallel", "parallel"),
        name="conv_ffn",
    )(x, x, g.reshape(1, d), w_up.astype(MXU_DTYPE), dw_w, dw_b.reshape(1, -1), w_down.astype(MXU_DTYPE),
      final_g.reshape(1, d))


def kernel(x, mem, mix_norm_g, w_in, cmp_pe, cmp_w1, cmp_w2, gmlp_ln_g, gmlp_ln_b, gmlp_ws, gmlp_bs, conv_dw_w, conv_dw_b, conv_ln_g, conv_ln_b, mix_out_g, w_out, xattn_norm_g, mem_norm_g, xattn_wq, xattn_wk, xattn_wv, xattn_wo, ffn_norm_g, ffn_w_up, ffn_dw_w, ffn_dw_b, ffn_w_down, final_norm_g):
    depth = w_in.shape[0]
    for l in range(depth):
        q, kv6, zg, zc, gl = _in_proj(x, mix_norm_g[l], _permute_w_in(w_in[l]))
        kv_cmp = _compress(kv6, *_pack_cmp_weights(cmp_pe[l], cmp_w1[l], cmp_w2[l]))
        y_nsa = _nsa(q, gl, kv_cmp, kv6)
        mix = dict(gmlp_ln_g=gmlp_ln_g[l], gmlp_ln_b=gmlp_ln_b[l], gmlp_ws=gmlp_ws[l], gmlp_bs=gmlp_bs[l],
                   conv_dw_w=conv_dw_w[l], conv_dw_b=conv_dw_b[l], conv_ln_g=conv_ln_g[l],
                   conv_ln_b=conv_ln_b[l], mix_out_g=mix_out_g[l], w_out=w_out[l])
        x = _mix_out(x, y_nsa, zg, zc, mix)
        mk, mv = _mem_kv(mem, mem_norm_g[l], xattn_wk[l], xattn_wv[l])
        x = _xattn(x, xattn_norm_g[l], xattn_wq[l], mk, mv, xattn_wo[l])
        x = _ffn(x, ffn_norm_g[l], ffn_w_up[l], ffn_dw_w[l], ffn_dw_b[l], ffn_w_down[l], final_norm_g,
                 final_norm=(l == depth - 1))
    return x
```

```python
import functools

import jax
import jax.numpy as jnp
from jax import lax
from jax.experimental import pallas as pl
from jax.experimental.pallas import tpu as pltpu

D_MODEL = 1024
NSA_HEADS = 8
NSA_KV_HEADS = 2
NSA_HEAD_DIM = 64
NSA_REP = NSA_HEADS // NSA_KV_HEADS
CMP_BLOCK = 32
CMP_STRIDE = 16
SEL_BLOCK = 64
N_SELECT = 16
WINDOW = 512
Q_BLOCK = 128
NSA_WIDTH = NSA_HEADS * NSA_HEAD_DIM
KV_WIDTH = NSA_KV_HEADS * NSA_HEAD_DIM
GMLP_WIDTH = 256
GMLP_GROUPS = 4
GMLP_CHUNK = 128
CONV_WIDTH = 256
CONV_TAPS = 31
XATTN_HEADS = 4
XATTN_HEAD_DIM = D_MODEL // XATTN_HEADS
D_FF = 2816
FFN_CONV_TAPS = 3
EPS = 1e-6
NEG_INF = -1e30
TINY = 1e-30
FORCE = 1e4
LOG2E = 1.4426950408889634
MAX_FLOOR = -1e29
KNOCKOUT = -3.0e38

LANES = 128
BF16_SUBLANES = 16
VMEM_LIMIT_BYTES = 48 * 1024 * 1024

MXU_DTYPE = jnp.bfloat16
F32 = jnp.float32

IN_PROJ_ROWS = 512
NSA_Q_TILE = 128
SEL_KEY_CHUNK = 512
MIX_ROWS = 512
CONV_HALO = 32
XATTN_ROWS = 512
FFN_ROWS = 512
FFN_COLS = 256
FFN_HALO = 16
GATE_PAD = LANES
KV_PACK = 2 * NSA_HEAD_DIM


def _params(*semantics):
    return pltpu.CompilerParams(dimension_semantics=semantics, vmem_limit_bytes=VMEM_LIMIT_BYTES)


def _dot(a, b):
    return jnp.dot(a.astype(MXU_DTYPE), b.astype(MXU_DTYPE), preferred_element_type=F32)


def _dot_nt(a, b):
    return lax.dot_general(a.astype(MXU_DTYPE), b.astype(MXU_DTYPE), (((1,), (1,)), ((), ())),
                           preferred_element_type=F32)


def _rms(x, g):
    return x * lax.rsqrt(jnp.mean(x * x, axis=-1, keepdims=True) + EPS) * g


def _layer_norm(x, g, b):
    mu = jnp.mean(x, axis=-1, keepdims=True)
    xc = x - mu
    var = jnp.mean(xc * xc, axis=-1, keepdims=True)
    return xc * lax.rsqrt(var + EPS) * g + b


def _gelu_tanh(x):
    return 0.5 * x * (1.0 + jnp.tanh(0.7978845608028654 * (x + 0.044715 * (x * x * x))))


def _masked_softmax(s, valid):
    s = jnp.where(valid, s, NEG_INF)
    m = jnp.max(s, axis=-1, keepdims=True)
    e = jnp.where(valid, jnp.exp(s - m), 0.0)
    return e / jnp.maximum(jnp.sum(e, axis=-1, keepdims=True), TINY)


IN_Q = (0, NSA_WIDTH)
IN_KV = (IN_Q[1], IN_Q[1] + 6 * KV_PACK)
IN_ZG = (IN_KV[1], IN_KV[1] + 2 * GMLP_WIDTH)
IN_ZC = (IN_ZG[1], IN_ZG[1] + 2 * CONV_WIDTH)
IN_GL = (IN_ZC[1], IN_ZC[1] + NSA_KV_HEADS * GATE_PAD)
IN_COLS = IN_GL[1]


def _permute_w_in(w):
    off_kv = NSA_WIDTH
    off_gl = NSA_WIDTH + 6 * KV_WIDTH
    off_zg = off_gl + NSA_HEADS * 3
    off_zc = off_zg + 2 * GMLP_WIDTH
    parts = [w[:, :NSA_WIDTH]]
    for branch in range(3):
        k0 = off_kv + (2 * branch) * KV_WIDTH
        v0 = off_kv + (2 * branch + 1) * KV_WIDTH
        for g in range(NSA_KV_HEADS):
            parts.append(w[:, k0 + g * NSA_HEAD_DIM:k0 + (g + 1) * NSA_HEAD_DIM])
            parts.append(w[:, v0 + g * NSA_HEAD_DIM:v0 + (g + 1) * NSA_HEAD_DIM])
    parts.append(w[:, off_zg:off_zg + 2 * GMLP_WIDTH])
    parts.append(w[:, off_zc:off_zc + 2 * CONV_WIDTH])
    n_gate = NSA_REP * 3
    for g in range(NSA_KV_HEADS):
        parts.append(jnp.pad(w[:, off_gl + g * n_gate:off_gl + (g + 1) * n_gate], ((0, 0), (0, GATE_PAD - n_gate))))
    return jnp.concatenate(parts, axis=1).astype(MXU_DTYPE)


def _in_proj_kernel(x_ref, g_ref, w_ref, q_ref, kv_ref, zg_ref, zc_ref, gl_ref):
    h = _rms(x_ref[...], g_ref[...]).astype(MXU_DTYPE)
    q_ref[...] = _dot(h, w_ref[:, IN_Q[0]:IN_Q[1]]).astype(q_ref.dtype)
    for j in range(6):
        c0 = IN_KV[0] + j * KV_PACK
        kv_ref[j] = _dot(h, w_ref[:, c0:c0 + KV_PACK]).astype(kv_ref.dtype)
    zg_ref[...] = _dot(h, w_ref[:, IN_ZG[0]:IN_ZG[1]]).astype(zg_ref.dtype)
    zc_ref[...] = _dot(h, w_ref[:, IN_ZC[0]:IN_ZC[1]]).astype(zc_ref.dtype)
    gl_ref[...] = _dot(h, w_ref[:, IN_GL[0]:IN_GL[1]])


def _in_proj(x, g, w):
    b, s, d = x.shape
    tm = IN_PROJ_ROWS
    return pl.pallas_call(
        _in_proj_kernel,
        grid=(b, s // tm),
        in_specs=[
            pl.BlockSpec((None, tm, d), lambda bb, i: (bb, i, 0)),
            pl.BlockSpec((1, d), lambda bb, i: (0, 0)),
            pl.BlockSpec((d, IN_COLS), lambda bb, i: (0, 0)),
        ],
        out_specs=[
            pl.BlockSpec((None, tm, NSA_WIDTH), lambda bb, i: (bb, i, 0)),
            pl.BlockSpec((None, 6, tm, KV_PACK), lambda bb, i: (bb, 0, i, 0)),
            pl.BlockSpec((None, tm, 2 * GMLP_WIDTH), lambda bb, i: (bb, i, 0)),
            pl.BlockSpec((None, tm, 2 * CONV_WIDTH), lambda bb, i: (bb, i, 0)),
            pl.BlockSpec((None, tm, NSA_KV_HEADS * GATE_PAD), lambda bb, i: (bb, i, 0)),
        ],
        out_shape=[
            jax.ShapeDtypeStruct((b, s, NSA_WIDTH), MXU_DTYPE),
            jax.ShapeDtypeStruct((b, 6, s, KV_PACK), MXU_DTYPE),
            jax.ShapeDtypeStruct((b, s, 2 * GMLP_WIDTH), MXU_DTYPE),
            jax.ShapeDtypeStruct((b, s, 2 * CONV_WIDTH), MXU_DTYPE),
            jax.ShapeDtypeStruct((b, s, NSA_KV_HEADS * GATE_PAD), F32),
        ],
        compiler_params=_params("parallel", "parallel"),
        name="in_proj",
    )(x, g.reshape(1, d), w)


CMP_ROW = CMP_STRIDE * KV_PACK


def _pack_cmp_weights(pe, w1, w2):
    dh = NSA_HEAD_DIM
    w1r = w1.reshape(2, CMP_BLOCK, dh, dh)
    z = jnp.zeros_like(w1r[0])
    wk = jnp.concatenate([w1r[0], z], axis=-1)
    wv = jnp.concatenate([z, w1r[1]], axis=-1)
    w1p = jnp.concatenate([wk, wv], axis=1).reshape(CMP_BLOCK * KV_PACK, KV_PACK)
    z2 = jnp.zeros_like(w2[0])
    w2p = jnp.concatenate([jnp.concatenate([w2[0], z2], axis=1), jnp.concatenate([z2, w2[1]], axis=1)], axis=0)
    pep = jnp.concatenate([pe[0], pe[1]], axis=-1).reshape(1, CMP_BLOCK * KV_PACK)
    pep = jnp.broadcast_to(pep, (8, CMP_BLOCK * KV_PACK))
    return pep.astype(MXU_DTYPE), w1p.astype(MXU_DTYPE), w2p.astype(MXU_DTYPE)


def _compress_kernel(c_ref, pe_ref, w1_ref, w2_ref, o_ref):
    c = c_ref[...]
    n = c.shape[0]
    first = _dot(c, w1_ref[:CMP_ROW, :])
    second = _dot(c, w1_ref[CMP_ROW:, :])
    pe_term = _dot(pe_ref[...], w1_ref[...])[0:1, :]
    h = first + pltpu.roll(second, n - 1, 0) + pe_term
    o_ref[...] = _dot(_gelu_tanh(h), w2_ref[...]).astype(o_ref.dtype)


def _compress(kv6, pep, w1p, w2p):
    b, _, s, _ = kv6.shape
    n = s // CMP_STRIDE
    c = kv6[:, :NSA_KV_HEADS].reshape(b, NSA_KV_HEADS, n, CMP_ROW)
    return pl.pallas_call(
        _compress_kernel,
        grid=(b, NSA_KV_HEADS),
        in_specs=[
            pl.BlockSpec((None, None, n, CMP_ROW), lambda bb, g: (bb, g, 0, 0)),
            pl.BlockSpec((8, 2 * CMP_ROW), lambda bb, g: (0, 0)),
            pl.BlockSpec((2 * CMP_ROW, KV_PACK), lambda bb, g: (0, 0)),
            pl.BlockSpec((KV_PACK, KV_PACK), lambda bb, g: (0, 0)),
        ],
        out_specs=pl.BlockSpec((None, None, n, KV_PACK), lambda bb, g: (bb, g, 0, 0)),
        out_shape=jax.ShapeDtypeStruct((b, NSA_KV_HEADS, n, KV_PACK), MXU_DTYPE),
        compiler_params=_params("parallel", "parallel"),
        name="nsa_compress",
    )(c, pep, w1p, w2p)


def _split3(x):
    a = x.astype(MXU_DTYPE)
    r = x - a.astype(F32)
    b = r.astype(MXU_DTYPE)
    c = (r - b.astype(F32)).astype(MXU_DTYPE)
    return a, b, c


def _softmax2_parts(s):
    m = jnp.maximum(jnp.max(s, axis=-1, keepdims=True), MAX_FLOOR)
    e = jnp.exp2(s - m)
    return e, 1.0 / jnp.maximum(jnp.sum(e, axis=-1, keepdims=True), TINY)


def _nsa_kernel(q_ref, gl_ref, cmp_ref, kvs_ref, kvw_ref, blk_ref, ovl_ref, diag_ref, o_ref, s_buf, e_buf, ovt_ref,
                *, seq):
    dh, qb, rep, groups = NSA_HEAD_DIM, q_ref.shape[0], NSA_REP, NSA_KV_HEADS
    n_cmp_rows = seq // CMP_STRIDE
    n_slc = seq // SEL_BLOCK
    k_sel = min(N_SELECT, n_slc)
    kc = min(SEL_KEY_CHUNK, seq)
    win_keys = WINDOW + qb
    sel_shift = SEL_BLOCK.bit_length() - 1
    gw = rep * dh

    bi = pl.program_id(1)
    t0 = bi * qb

    @pl.when(bi == 0)
    def _():
        chunk_lanes = lax.broadcasted_iota(jnp.int32, (kc, KV_PACK), 1)
        for g in range(groups):
            for c in range(seq // kc):
                ov = jnp.where(chunk_lanes < dh, 1.0, kvs_ref[g, c * kc:(c + 1) * kc, :].astype(F32))
                ovt_ref[g, c] = ov.T.astype(ovt_ref.dtype)

    t_col = t0 + lax.broadcasted_iota(jnp.int32, (qb, 1), 0)
    t_row = t0 + lax.broadcasted_iota(jnp.int32, (1, qb), 1)
    lane = lax.broadcasted_iota(jnp.int32, (qb, KV_PACK), 1)

    prepared = _nsa_prepare(q_ref, cmp_ref, kvs_ref, kvw_ref, ovl_ref, diag_ref, t0, t_col, t_row, lane,
                            seq=seq, qb=qb)
    q_aug = [p[0] for p in prepared]

    def scores(g, c, slot):
        k0 = pl.multiple_of(c * kc, kc)
        keys = jnp.concatenate([kvs_ref[g, pl.ds(k0, kc), :], blk_ref[pl.ds(k0, kc), :]], axis=1)
        s_buf[g, slot] = _dot_nt(keys, q_aug[g])

    def softmax_step(g, slot, m):
        s = s_buf[g, slot]
        m_new = jnp.maximum(m, jnp.max(s, axis=0, keepdims=True))
        e_buf[g, slot] = jnp.exp2(s - m_new).astype(e_buf.dtype)
        return m_new, jnp.exp2(m - m_new)

    def weighted_values(g, c, slot, alpha, acc):
        return alpha * acc + _dot(ovt_ref[g, c], e_buf[g, slot])

    last_chunk = seq // kc - 1

    def pair_body(i, carry):
        c = 2 * i
        heads = range(groups)
        m, acc, alpha_prev = ([carry[g][k] for g in heads] for k in range(3))
        alpha_even, alpha_odd = [None] * groups, [None] * groups
        for g in heads:
            scores(g, c + 1, 1)
        for g in heads:
            m[g], alpha_even[g] = softmax_step(g, 0, m[g])
        for g in heads:
            acc[g] = weighted_values(g, jnp.maximum(c - 1, 0), 1, alpha_prev[g], acc[g])
        for g in heads:
            scores(g, jnp.minimum(c + 2, last_chunk), 0)
        for g in heads:
            m[g], alpha_odd[g] = softmax_step(g, 1, m[g])
        for g in heads:
            acc[g] = weighted_values(g, c, 0, alpha_even[g], acc[g])
        return tuple((m[g], acc[g], alpha_odd[g]) for g in heads)

    n_pairs = ((t0 + qb - SEL_BLOCK + kc - 1) // kc + 1) // 2
    for g in range(groups):
        scores(g, 0, 0)
        e_buf[g, 1] = jnp.zeros(e_buf.shape[2:], e_buf.dtype)
    init = tuple((p[3], p[4], jnp.ones_like(p[3])) for p in prepared)
    final = lax.fori_loop(0, n_pairs, pair_body, init)

    lane4 = lax.broadcasted_iota(jnp.int32, (rep * qb, KV_PACK), 1)
    for g in range(groups):
        _, o_c, o_w, _, _ = prepared[g]
        _, acc_s, alpha_last = final[g]
        acc_s = weighted_values(g, 2 * n_pairs - 1, 1, alpha_last, acc_s).T
        o_s = acc_s / jnp.where(lane4 < dh, 1.0, pltpu.roll(acc_s, dh, 1))

        gates = jax.nn.sigmoid(gl_ref[:, g * GATE_PAD:(g + 1) * GATE_PAD])
        outs = []
        for r in range(rep):
            rows = slice(r * qb, (r + 1) * qb)
            outs.append(gates[:, 3 * r:3 * r + 1] * o_c[rows] + gates[:, 3 * r + 1:3 * r + 2] * o_s[rows]
                        + gates[:, 3 * r + 2:3 * r + 3] * o_w[rows])
        for p in range(rep // 2):
            pair = jnp.where(lane < dh, pltpu.roll(outs[2 * p], dh, 1), outs[2 * p + 1])
            c0 = g * gw + p * KV_PACK
            o_ref[:, c0:c0 + KV_PACK] = pair.astype(o_ref.dtype)


def _nsa_prepare(q_ref, cmp_ref, kvs_ref, kvw_ref, ovl_ref, diag_ref, t0, t_col, t_row, lane, *, seq, qb):
    dh, rep = NSA_HEAD_DIM, NSA_REP
    heads = range(NSA_KV_HEADS)
    n_cmp_rows = seq // CMP_STRIDE
    n_slc = seq // SEL_BLOCK
    k_sel = min(N_SELECT, n_slc)
    win_keys = WINDOW + qb
    sel_shift = SEL_BLOCK.bit_length() - 1
    gw = rep * dh

    q4 = []
    for g in heads:
        parts = []
        for p in range(rep // 2):
            c0 = g * gw + p * KV_PACK
            pair = q_ref[:, c0:c0 + KV_PACK].astype(F32) * (dh ** -0.5 * LOG2E)
            parts.append(jnp.where(lane < dh, pair, 0.0))
            parts.append(jnp.where(lane < dh, pltpu.roll(pair, dh, 1), 0.0))
        q4.append(jnp.concatenate(parts, axis=0).astype(MXU_DTYPE))

    n_idx = lax.broadcasted_iota(jnp.int32, (1, n_cmp_rows), 1)
    valid_c = (n_idx * CMP_STRIDE + (CMP_BLOCK - 1) <= t_col) & (n_idx < n_cmp_rows - 1)
    bias_c = jnp.where(valid_c, 0.0, NEG_INF)
    o_c, p_sum = [], []
    for g in heads:
        kvc = cmp_ref[g]
        e_c, inv_c = _softmax2_parts(_dot_nt(q4[g], kvc).reshape(rep, qb, n_cmp_rows) + bias_c[None])
        p_c = e_c * inv_c
        o_c.append(_dot(p_c.reshape(rep * qb, n_cmp_rows), kvc))
        total = p_c[0]
        for r in range(1, rep):
            total = total + p_c[r]
        p_sum.append(total)

    jj = lax.broadcasted_iota(jnp.int32, (n_slc, qb), 0)
    cur = t_row >> sel_shift
    forced = (jj == 0) | (jj == cur) | (jj == cur - 1)
    causal = jj <= cur
    jf = jj.astype(F32)
    imp = []
    for g in heads:
        acc = None
        for term in _split3(p_sum[g]):
            d = _dot_nt(ovl_ref[...], term)
            acc = d if acc is None else acc + d
        imp.append(jnp.where(causal, jnp.where(forced, FORCE, acc), -FORCE))
    sel = [jnp.zeros((n_slc, qb), F32) for _ in heads]
    for _ in range(k_sel):
        for g in heads:
            mx = jnp.max(imp[g], axis=0, keepdims=True)
            first = jnp.min(jnp.where(imp[g] == mx, jf, float(n_slc)), axis=0, keepdims=True)
            pick = jf == first
            sel[g] = jnp.where(pick, 1.0, sel[g])
            imp[g] = jnp.where(pick, KNOCKOUT, imp[g])
    before_own = jj < cur
    q_aug = []
    for g in heads:
        bias_t = jnp.where((sel[g] > 0.5) & before_own, 0.0, NEG_INF)
        neg_sel = bias_t.T.astype(MXU_DTYPE)
        q_aug.append(jnp.concatenate([q4[g], jnp.concatenate([neg_sel] * rep, axis=0)], axis=1))

    w0 = pl.multiple_of(jnp.maximum(t0 - WINDOW, 0), qb)
    kp = w0 + lax.broadcasted_iota(jnp.int32, (1, win_keys), 1)
    bias_w = jnp.where((kp <= t_col) & (kp > t_col - WINDOW), 0.0, NEG_INF)
    o_w = []
    for g in heads:
        kvw = kvw_ref[g, pl.ds(w0, win_keys), :]
        e_w, inv_w = _softmax2_parts(_dot_nt(q4[g], kvw).reshape(rep, qb, win_keys) + bias_w[None])
        o_w.append(_dot(e_w.reshape(rep * qb, win_keys), kvw) * inv_w.reshape(rep * qb, 1))

    m0, acc0 = [], []
    for g in heads:
        kv_own = kvs_ref[g, pl.ds(pl.multiple_of(t0, qb), qb), :]
        ov_own_t = jnp.where(lane < dh, 1.0, kv_own.astype(F32)).T
        s_d = _dot_nt(kv_own, q4[g]) + diag_ref[...]
        m0.append(jnp.max(s_d, axis=0, keepdims=True))
        acc0.append(_dot(ov_own_t, jnp.exp2(s_d - m0[g])))
    return [(q_aug[g], o_c[g], o_w[g], m0[g], acc0[g]) for g in heads]


def _nsa(q, gl, kv_cmp, kv6):
    b, s, _ = q.shape
    qt = min(NSA_Q_TILE, s)
    nb = s // qt
    gw = NSA_REP * NSA_HEAD_DIM
    n_cmp_rows = s // CMP_STRIDE
    n_slc = s // SEL_BLOCK
    kc = min(SEL_KEY_CHUNK, s)
    assert (s // kc) % 2 == 0, "the chunk loop walks key chunks in pairs"
    key_block = jnp.arange(s, dtype=jnp.int32)[:, None] // SEL_BLOCK
    blk_onehot = (key_block == jnp.arange(n_slc, dtype=jnp.int32)[None, :]).astype(MXU_DTYPE)
    sj = jnp.arange(n_slc, dtype=jnp.int32)[:, None] * SEL_BLOCK
    ci = jnp.arange(n_cmp_rows, dtype=jnp.int32)[None, :] * CMP_STRIDE
    overlap_t = ((ci < sj + SEL_BLOCK) & (ci + CMP_BLOCK > sj)).astype(MXU_DTYPE)
    ti = jnp.arange(qt, dtype=jnp.int32)
    own_block = (ti[:, None] // SEL_BLOCK == ti[None, :] // SEL_BLOCK) & (ti[:, None] <= ti[None, :])
    diag_bias = jnp.tile(jnp.where(own_block, 0.0, NEG_INF).astype(F32), (1, NSA_REP))
    ng = NSA_KV_HEADS
    const = lambda shape: pl.BlockSpec(shape, lambda bb, i: (0,) * len(shape), pipeline_mode=pl.Buffered(1))
    return pl.pallas_call(
        functools.partial(_nsa_kernel, seq=s),
        grid=(b, nb),
        in_specs=[
            pl.BlockSpec((None, qt, ng * gw), lambda bb, i: (bb, i, 0)),
            pl.BlockSpec((None, qt, ng * GATE_PAD), lambda bb, i: (bb, i, 0)),
            pl.BlockSpec((None, ng, n_cmp_rows, KV_PACK), lambda bb, i: (bb, 0, 0, 0)),
            pl.BlockSpec((None, ng, s, KV_PACK), lambda bb, i: (bb, 1, 0, 0)),
            pl.BlockSpec((None, ng, s, KV_PACK), lambda bb, i: (bb, 2, 0, 0)),
            const((s, n_slc)), const((n_slc, n_cmp_rows)), const((qt, NSA_REP * qt)),
        ],
        out_specs=pl.BlockSpec((None, qt, ng * gw), lambda bb, i: (bb, i, 0)),
        out_shape=jax.ShapeDtypeStruct((b, s, NSA_WIDTH), MXU_DTYPE),
        scratch_shapes=[pltpu.VMEM((ng, 2, kc, NSA_REP * qt), F32),
                        pltpu.VMEM((ng, 2, kc, NSA_REP * qt), MXU_DTYPE),
                        pltpu.VMEM((ng, s // kc, KV_PACK, kc), MXU_DTYPE)],
        compiler_params=_params("parallel", "arbitrary"),
        name="nsa_attention",
    )(q, gl, kv_cmp, kv6, kv6, blk_onehot, overlap_t, diag_bias)


def _mix_out_kernel(x_ref, nsa_ref, zg_ref, zc_ref, zch_ref, gln_g_ref, gln_b_ref, ws_ref, bs_ref,
                    dww_ref, dwb_ref, cln_g_ref, cln_b_ref, og_ref, wo_ref, o_ref, hbuf_ref):
    tm = x_ref.shape[0]
    i = pl.program_id(1)

    def glu(z):
        z = z.astype(F32)
        return z[:, :CONV_WIDTH] * jax.nn.sigmoid(z[:, CONV_WIDTH:])

    halo = glu(zch_ref[...])
    hbuf_ref[0:CONV_HALO, :] = jnp.where(i == 0, 0.0, halo)
    hbuf_ref[CONV_HALO:CONV_HALO + tm, :] = glu(zc_ref[...])
    conv = jnp.broadcast_to(dwb_ref[...], (tm, CONV_WIDTH))
    for k in range(CONV_TAPS):
        off = CONV_HALO - (CONV_TAPS - 1) + k
        conv = conv + dww_ref[k:k + 1, :] * hbuf_ref[off:off + tm, :]
    y_conv = jax.nn.silu(_layer_norm(conv, cln_g_ref[...], cln_b_ref[...]))

    zg = _gelu_tanh(zg_ref[...].astype(F32))
    u = zg[:, :GMLP_WIDTH]
    v = _layer_norm(zg[:, GMLP_WIDTH:], gln_g_ref[...], gln_b_ref[...]).astype(MXU_DTYPE)
    row = lax.broadcasted_iota(jnp.int32, (GMLP_CHUNK, GMLP_CHUNK), 0)
    col = lax.broadcasted_iota(jnp.int32, (GMLP_CHUNK, GMLP_CHUNK), 1)
    grp = lax.broadcasted_iota(jnp.int32, (GMLP_CHUNK, GMLP_WIDTH), 1) // (GMLP_WIDTH // GMLP_GROUPS)
    w_causal = [jnp.where(row >= col, ws_ref[g], 0.0).astype(MXU_DTYPE) for g in range(GMLP_GROUPS)]
    bias = jnp.zeros((GMLP_CHUNK, GMLP_WIDTH), F32)
    for g in range(GMLP_GROUPS):
        bias = jnp.where(grp == g, bs_ref[:, g:g + 1], bias)
    mixed = []
    for c in range(tm // GMLP_CHUNK):
        vc = v[c * GMLP_CHUNK:(c + 1) * GMLP_CHUNK]
        acc = bias
        for g in range(GMLP_GROUPS):
            acc = acc + jnp.where(grp == g, jnp.dot(w_causal[g], vc, preferred_element_type=F32), 0.0)
        mixed.append(acc)
    y_gmlp = u * jnp.concatenate(mixed, axis=0)

    og = og_ref[...]
    n0, n1 = NSA_WIDTH, NSA_WIDTH + GMLP_WIDTH
    y = _dot(_rms(nsa_ref[...].astype(F32), og[:, :n0]), wo_ref[:n0, :])
    y = y + _dot(_rms(y_gmlp, og[:, n0:n1]), wo_ref[n0:n1, :])
    y = y + _dot(_rms(y_conv, og[:, n1:]), wo_ref[n1:, :])
    o_ref[...] = x_ref[...] + y


def _mix_out(x, y_nsa, zg, zc, p):
    b, s, d = x.shape
    tm = min(MIX_ROWS, s)
    halo_blocks = tm // CONV_HALO
    row2 = lambda a: a.reshape(1, -1)
    full = lambda shape: pl.BlockSpec(shape, lambda bb, i: (0,) * len(shape))
    tile = lambda w: pl.BlockSpec((None, tm, w), lambda bb, i: (bb, i, 0))
    return pl.pallas_call(
        _mix_out_kernel,
        grid=(b, s // tm),
        in_specs=[
            tile(d), tile(NSA_WIDTH), tile(2 * GMLP_WIDTH), tile(2 * CONV_WIDTH),
            pl.BlockSpec((None, CONV_HALO, 2 * CONV_WIDTH),
                         lambda bb, i: (bb, jnp.maximum(i * halo_blocks - 1, 0), 0)),
            full((1, GMLP_WIDTH)), full((1, GMLP_WIDTH)),
            full((GMLP_GROUPS, GMLP_CHUNK, GMLP_CHUNK)), full((GMLP_CHUNK, GMLP_GROUPS)),
            full((CONV_TAPS, CONV_WIDTH)), full((1, CONV_WIDTH)), full((1, CONV_WIDTH)), full((1, CONV_WIDTH)),
            full((1, d)), full((d, d)),
        ],
        out_specs=tile(d),
        out_shape=jax.ShapeDtypeStruct((b, s, d), F32),
        scratch_shapes=[pltpu.VMEM((CONV_HALO + tm, CONV_WIDTH), F32)],
        compiler_params=_params("parallel", "parallel"),
        name="mix_out",
    )(x, y_nsa, zg, zc, zc, row2(p["gmlp_ln_g"]), row2(p["gmlp_ln_b"]), p["gmlp_ws"], p["gmlp_bs"].T,
      p["conv_dw_w"], row2(p["conv_dw_b"]), row2(p["conv_ln_g"]), row2(p["conv_ln_b"]),
      row2(p["mix_out_g"]), p["w_out"].astype(MXU_DTYPE))


def _mem_kv_kernel(m_ref, g_ref, wk_ref, wv_ref, k_ref, v_ref):
    m = _rms(m_ref[...], g_ref[...]).astype(MXU_DTYPE)
    k_ref[...] = _dot(m, wk_ref[...]).astype(k_ref.dtype)
    v_ref[...] = _dot(m, wv_ref[...]).astype(v_ref.dtype)


def _mem_kv(mem, g, wk, wv):
    b, n, d = mem.shape
    blk = pl.BlockSpec((None, n, d), lambda bb: (bb, 0, 0))
    wspec = pl.BlockSpec((d, d), lambda bb: (0, 0))
    return pl.pallas_call(
        _mem_kv_kernel,
        grid=(b,),
        in_specs=[blk, pl.BlockSpec((1, d), lambda bb: (0, 0)), wspec, wspec],
        out_specs=[blk, blk],
        out_shape=[jax.ShapeDtypeStruct((b, n, d), MXU_DTYPE)] * 2,
        compiler_params=_params("parallel"),
        name="xattn_mem_kv",
    )(mem, g.reshape(1, d), wk.astype(MXU_DTYPE), wv.astype(MXU_DTYPE))


def _xattn_kernel(x_ref, g_ref, wq_ref, k_ref, v_ref, wo_ref, o_ref):
    dh = XATTN_HEAD_DIM
    x = x_ref[...]
    h = _rms(x, g_ref[...]).astype(MXU_DTYPE)
    y = x
    for hd in range(XATTN_HEADS):
        cols = slice(hd * dh, (hd + 1) * dh)
        q = _dot(h, wq_ref[:, cols]) * (dh ** -0.5)
        s = _dot_nt(q, k_ref[:, cols])
        e = jnp.exp(s - jnp.max(s, axis=-1, keepdims=True))
        p = e / jnp.sum(e, axis=-1, keepdims=True)
        y = y + _dot(_dot(p, v_ref[:, cols]), wo_ref[cols, :])
    o_ref[...] = y


def _xattn(x, g, wq, k, v, wo):
    b, s, d = x.shape
    n = k.shape[1]
    tm = min(XATTN_ROWS, s)
    tile = pl.BlockSpec((None, tm, d), lambda bb, i: (bb, i, 0))
    wspec = pl.BlockSpec((d, d), lambda bb, i: (0, 0))
    mspec = pl.BlockSpec((None, n, d), lambda bb, i: (bb, 0, 0))
    return pl.pallas_call(
        _xattn_kernel,
        grid=(b, s // tm),
        in_specs=[tile, pl.BlockSpec((1, d), lambda bb, i: (0, 0)), wspec, mspec, mspec, wspec],
        out_specs=tile,
        out_shape=jax.ShapeDtypeStruct((b, s, d), F32),
        compiler_params=_params("parallel", "parallel"),
        name="xattn",
    )(x, g.reshape(1, d), wq.astype(MXU_DTYPE), k, v, wo.astype(MXU_DTYPE))


def _ffn_kernel(x_ref, xh_ref, g_ref, wup_ref, cw_ref, cb_ref, wd_ref, fg_ref, o_ref, xn_ref, h_buf, act_buf,
                *, final_norm):
    tm = x_ref.shape[0]
    tc = FFN_COLS
    i = pl.program_id(1)
    halo = jnp.where(i == 0, 0.0, xh_ref[...])
    xn_ref[0:FFN_HALO, :] = _rms(halo, g_ref[...]).astype(xn_ref.dtype)
    xn_ref[FFN_HALO:FFN_HALO + tm, :] = _rms(x_ref[...], g_ref[...]).astype(xn_ref.dtype)
    xn = xn_ref[...]

    def conv(slot, half, c0):
        cols = slice(c0, c0 + tc)
        y = cb_ref[:, cols]
        for k in range(FFN_CONV_TAPS):
            shift = FFN_CONV_TAPS - 1 - k
            y = y + cw_ref[k:k + 1, cols] * h_buf[slot, half, FFN_HALO - shift:FFN_HALO - shift + tm, :]
        return y

    for j in range(D_FF // tc):
        slot = j % 2
        gate_c0, up_c0 = j * tc, D_FF + j * tc
        h_buf[slot, 0] = _dot(xn, wup_ref[:, gate_c0:gate_c0 + tc])
        h_buf[slot, 1] = _dot(xn, wup_ref[:, up_c0:up_c0 + tc])
        act = jax.nn.silu(conv(slot, 0, gate_c0)) * conv(slot, 1, up_c0)
        act_buf[:, j * tc:(j + 1) * tc] = act.astype(act_buf.dtype)

    y = x_ref[...] + _dot(act_buf[...], wd_ref[...])
    o_ref[...] = _rms(y, fg_ref[...]) if final_norm else y


def _ffn(x, g, w_up, dw_w, dw_b, w_down, final_g, final_norm):
    b, s, d = x.shape
    tm = min(FFN_ROWS, s)
    halo_blocks = tm // FFN_HALO
    tile = pl.BlockSpec((None, tm, d), lambda bb, i: (bb, i, 0))
    resident = lambda shape: pl.BlockSpec(shape, lambda bb, i: (0,) * len(shape), pipeline_mode=pl.Buffered(1))
    return pl.pallas_call(
        functools.partial(_ffn_kernel, final_norm=final_norm),
        grid=(b, s // tm),
        in_specs=[
            tile,
            pl.BlockSpec((None, FFN_HALO, d), lambda bb, i: (bb, jnp.maximum(i * halo_blocks - 1, 0), 0)),
            resident((1, d)),
            resident((d, 2 * D_FF)),
            resident((FFN_CONV_TAPS, 2 * D_FF)),
            resident((1, 2 * D_FF)),
            resident((D_FF, d)),
            resident((1, d)),
        ],
        out_specs=tile,
        out_shape=jax.ShapeDtypeStruct((b, s, d), F32),
        scratch_shapes=[pltpu.VMEM((FFN_HALO + tm, d), MXU_DTYPE),
                        pltpu.VMEM((2, 2, FFN_HALO + tm, FFN_COLS), F32),
                        pltpu.VMEM((tm, D_FF), MXU_DTYPE)],
        compiler_params=_params("parallel", "parallel"),
        name="conv_ffn",
    )(x, x, g.reshape(1, d), w_up.astype(MXU_DTYPE), dw_w, dw_b.reshape(1, -1), w_down.astype(MXU_DTYPE),
      final_g.reshape(1, d))


def kernel(x, mem, mix_norm_g, w_in, cmp_pe, cmp_w1, cmp_w2, gmlp_ln_g, gmlp_ln_b, gmlp_ws, gmlp_bs, conv_dw_w, conv_dw_b, conv_ln_g, conv_ln_b, mix_out_g, w_out, xattn_norm_g, mem_norm_g, xattn_wq, xattn_wk, xattn_wv, xattn_wo, ffn_norm_g, ffn_w_up, ffn_dw_w, ffn_dw_b, ffn_w_down, final_norm_g):
    depth = w_in.shape[0]
    for l in range(depth):
        q, kv6, zg, zc, gl = _in_proj(x, mix_norm_g[l], _permute_w_in(w_in[l]))
        kv_cmp = _compress(kv6, *_pack_cmp_weights(cmp_pe[l], cmp_w1[l], cmp_w2[l]))
        y_nsa = _nsa(q, gl, kv_cmp, kv6)
        mix = dict(gmlp_ln_g=gmlp_ln_g[l], gmlp_ln_b=gmlp_ln_b[l], gmlp_ws=gmlp_ws[l], gmlp_bs=gmlp_bs[l],
                   conv_dw_w=conv_dw_w[l], conv_dw_b=conv_dw_b[l], conv_ln_g=conv_ln_g[l],
                   conv_ln_b=conv_ln_b[l], mix_out_g=mix_out_g[l], w_out=w_out[l])
        x = _mix_out(x, y_nsa, zg, zc, mix)
        mk, mv = _mem_kv(mem, mem_norm_g[l], xattn_wk[l], xattn_wv[l])
        x = _xattn(x, xattn_norm_g[l], xattn_wq[l], mk, mv, xattn_wo[l])
        x = _ffn(x, ffn_norm_g[l], ffn_w_up[l], ffn_dw_w[l], ffn_dw_b[l], ffn_w_down[l], final_norm_g,
                 final_norm=(l == depth - 1))
    return x
```

```python
import functools

import jax
import jax.numpy as jnp
from jax import lax
from jax.experimental import pallas as pl
from jax.experimental.pallas import tpu as pltpu

D_MODEL = 1024
NSA_HEADS = 8
NSA_KV_HEADS = 2
NSA_HEAD_DIM = 64
NSA_REP = NSA_HEADS // NSA_KV_HEADS
CMP_BLOCK = 32
CMP_STRIDE = 16
SEL_BLOCK = 64
N_SELECT = 16
WINDOW = 512
Q_BLOCK = 128
NSA_WIDTH = NSA_HEADS * NSA_HEAD_DIM
KV_WIDTH = NSA_KV_HEADS * NSA_HEAD_DIM
GMLP_WIDTH = 256
GMLP_GROUPS = 4
GMLP_CHUNK = 128
CONV_WIDTH = 256
CONV_TAPS = 31
XATTN_HEADS = 4
XATTN_HEAD_DIM = D_MODEL // XATTN_HEADS
D_FF = 2816
FFN_CONV_TAPS = 3
EPS = 1e-6
NEG_INF = -1e30
TINY = 1e-30
FORCE = 1e4
LOG2E = 1.4426950408889634
MAX_FLOOR = -1e29
KNOCKOUT = -3.0e38

LANES = 128
F32_SUBLANES = 8
BF16_SUBLANES = 16
VMEM_LIMIT_BYTES = 48 * 1024 * 1024

MXU_DTYPE = jnp.bfloat16
F32 = jnp.float32

IN_PROJ_ROWS = 512
NSA_Q_TILE = 128
SEL_KEY_CHUNK = 512
MIX_ROWS = 512
CONV_HALO = 32
XATTN_ROWS = 512
FFN_ROWS = 512
FFN_COLS = 256
FFN_HALO = 16
GATE_PAD = LANES
KV_PACK = 2 * NSA_HEAD_DIM


def _params(*semantics):
    return pltpu.CompilerParams(dimension_semantics=semantics, vmem_limit_bytes=VMEM_LIMIT_BYTES)


def _dot(a, b):
    return jnp.dot(a.astype(MXU_DTYPE), b.astype(MXU_DTYPE), preferred_element_type=F32)


def _dot_nt(a, b):
    return lax.dot_general(a.astype(MXU_DTYPE), b.astype(MXU_DTYPE), (((1,), (1,)), ((), ())),
                           preferred_element_type=F32)


def _rms(x, g):
    return x * lax.rsqrt(jnp.mean(x * x, axis=-1, keepdims=True) + EPS) * g


def _layer_norm(x, g, b):
    mu = jnp.mean(x, axis=-1, keepdims=True)
    xc = x - mu
    var = jnp.mean(xc * xc, axis=-1, keepdims=True)
    return xc * lax.rsqrt(var + EPS) * g + b


def _gelu_tanh(x):
    return 0.5 * x * (1.0 + jnp.tanh(0.7978845608028654 * (x + 0.044715 * (x * x * x))))


def _masked_softmax(s, valid):
    s = jnp.where(valid, s, NEG_INF)
    m = jnp.max(s, axis=-1, keepdims=True)
    e = jnp.where(valid, jnp.exp(s - m), 0.0)
    return e / jnp.maximum(jnp.sum(e, axis=-1, keepdims=True), TINY)


IN_Q = (0, NSA_WIDTH)
IN_KV = (IN_Q[1], IN_Q[1] + 6 * KV_PACK)
IN_ZG = (IN_KV[1], IN_KV[1] + 2 * GMLP_WIDTH)
IN_ZC = (IN_ZG[1], IN_ZG[1] + 2 * CONV_WIDTH)
IN_GL = (IN_ZC[1], IN_ZC[1] + GATE_PAD)
IN_COLS = IN_GL[1]


def _permute_w_in(w):
    off_kv = NSA_WIDTH
    off_gl = NSA_WIDTH + 6 * KV_WIDTH
    off_zg = off_gl + NSA_HEADS * 3
    off_zc = off_zg + 2 * GMLP_WIDTH
    parts = [w[:, :NSA_WIDTH]]
    for branch in range(3):
        k0 = off_kv + (2 * branch) * KV_WIDTH
        v0 = off_kv + (2 * branch + 1) * KV_WIDTH
        for g in range(NSA_KV_HEADS):
            parts.append(w[:, k0 + g * NSA_HEAD_DIM:k0 + (g + 1) * NSA_HEAD_DIM])
            parts.append(w[:, v0 + g * NSA_HEAD_DIM:v0 + (g + 1) * NSA_HEAD_DIM])
    parts.append(w[:, off_zg:off_zg + 2 * GMLP_WIDTH])
    parts.append(w[:, off_zc:off_zc + 2 * CONV_WIDTH])
    n_gate = NSA_HEADS * 3
    parts.append(jnp.pad(w[:, off_gl:off_gl + n_gate], ((0, 0), (0, GATE_PAD - n_gate))))
    return jnp.concatenate(parts, axis=1).astype(MXU_DTYPE)


def _in_proj_kernel(x_ref, g_ref, w_ref, q_ref, kv_ref, zg_ref, zc_ref, gl_ref):
    h = _rms(x_ref[...], g_ref[...]).astype(MXU_DTYPE)
    q_ref[...] = _dot(h, w_ref[:, IN_Q[0]:IN_Q[1]]).astype(q_ref.dtype)
    for j in range(6):
        c0 = IN_KV[0] + j * KV_PACK
        kv_ref[j] = _dot(h, w_ref[:, c0:c0 + KV_PACK]).astype(kv_ref.dtype)
    zg_ref[...] = _dot(h, w_ref[:, IN_ZG[0]:IN_ZG[1]]).astype(zg_ref.dtype)
    zc_ref[...] = _dot(h, w_ref[:, IN_ZC[0]:IN_ZC[1]]).astype(zc_ref.dtype)
    gl_ref[...] = _dot(h, w_ref[:, IN_GL[0]:IN_GL[1]])


def _in_proj(x, g, w):
    b, s, d = x.shape
    tm = IN_PROJ_ROWS
    return pl.pallas_call(
        _in_proj_kernel,
        grid=(b, s // tm),
        in_specs=[
            pl.BlockSpec((None, tm, d), lambda bb, i: (bb, i, 0)),
            pl.BlockSpec((1, d), lambda bb, i: (0, 0)),
            pl.BlockSpec((d, IN_COLS), lambda bb, i: (0, 0)),
        ],
        out_specs=[
            pl.BlockSpec((None, tm, NSA_WIDTH), lambda bb, i: (bb, i, 0)),
            pl.BlockSpec((None, 6, tm, KV_PACK), lambda bb, i: (bb, 0, i, 0)),
            pl.BlockSpec((None, tm, 2 * GMLP_WIDTH), lambda bb, i: (bb, i, 0)),
            pl.BlockSpec((None, tm, 2 * CONV_WIDTH), lambda bb, i: (bb, i, 0)),
            pl.BlockSpec((None, tm, GATE_PAD), lambda bb, i: (bb, i, 0)),
        ],
        out_shape=[
            jax.ShapeDtypeStruct((b, s, NSA_WIDTH), MXU_DTYPE),
            jax.ShapeDtypeStruct((b, 6, s, KV_PACK), MXU_DTYPE),
            jax.ShapeDtypeStruct((b, s, 2 * GMLP_WIDTH), MXU_DTYPE),
            jax.ShapeDtypeStruct((b, s, 2 * CONV_WIDTH), MXU_DTYPE),
            jax.ShapeDtypeStruct((b, s, GATE_PAD), F32),
        ],
        compiler_params=_params("parallel", "parallel"),
        name="in_proj",
    )(x, g.reshape(1, d), w)


CMP_ROW = CMP_STRIDE * KV_PACK


def _pack_cmp_weights(pe, w1, w2):
    dh = NSA_HEAD_DIM
    w1r = w1.reshape(2, CMP_BLOCK, dh, dh)
    z = jnp.zeros_like(w1r[0])
    wk = jnp.concatenate([w1r[0], z], axis=-1)
    wv = jnp.concatenate([z, w1r[1]], axis=-1)
    w1p = jnp.concatenate([wk, wv], axis=1).reshape(CMP_BLOCK * KV_PACK, KV_PACK)
    z2 = jnp.zeros_like(w2[0])
    w2p = jnp.concatenate([jnp.concatenate([w2[0], z2], axis=1), jnp.concatenate([z2, w2[1]], axis=1)], axis=0)
    pep = jnp.concatenate([pe[0], pe[1]], axis=-1).reshape(1, CMP_BLOCK * KV_PACK)
    pep = jnp.broadcast_to(pep, (8, CMP_BLOCK * KV_PACK))
    return pep.astype(MXU_DTYPE), w1p.astype(MXU_DTYPE), w2p.astype(MXU_DTYPE)


def _compress_kernel(c_ref, pe_ref, w1_ref, w2_ref, o_ref):
    c = c_ref[...]
    n = c.shape[0]
    first = _dot(c, w1_ref[:CMP_ROW, :])
    second = _dot(c, w1_ref[CMP_ROW:, :])
    pe_term = _dot(pe_ref[...], w1_ref[...])[0:1, :]
    h = first + pltpu.roll(second, n - 1, 0) + pe_term
    o_ref[...] = _dot(_gelu_tanh(h), w2_ref[...]).astype(o_ref.dtype)


def _compress(kv6, pep, w1p, w2p):
    b, _, s, _ = kv6.shape
    n = s // CMP_STRIDE
    c = kv6[:, :NSA_KV_HEADS].reshape(b, NSA_KV_HEADS, n, CMP_ROW)
    return pl.pallas_call(
        _compress_kernel,
        grid=(b, NSA_KV_HEADS),
        in_specs=[
            pl.BlockSpec((None, None, n, CMP_ROW), lambda bb, g: (bb, g, 0, 0)),
            pl.BlockSpec((8, 2 * CMP_ROW), lambda bb, g: (0, 0)),
            pl.BlockSpec((2 * CMP_ROW, KV_PACK), lambda bb, g: (0, 0)),
            pl.BlockSpec((KV_PACK, KV_PACK), lambda bb, g: (0, 0)),
        ],
        out_specs=pl.BlockSpec((None, None, n, KV_PACK), lambda bb, g: (bb, g, 0, 0)),
        out_shape=jax.ShapeDtypeStruct((b, NSA_KV_HEADS, n, KV_PACK), MXU_DTYPE),
        compiler_params=_params("parallel", "parallel"),
        name="nsa_compress",
    )(c, pep, w1p, w2p)


def _split3(x):
    a = x.astype(MXU_DTYPE)
    r = x - a.astype(F32)
    b = r.astype(MXU_DTYPE)
    c = (r - b.astype(F32)).astype(MXU_DTYPE)
    return a, b, c


def _softmax2_parts(s):
    m = jnp.maximum(jnp.max(s, axis=-1, keepdims=True), MAX_FLOOR)
    e = jnp.exp2(s - m)
    return e, 1.0 / jnp.maximum(jnp.sum(e, axis=-1, keepdims=True), TINY)


def _nsa_kernel(q_ref, gl_ref, cmp_ref, kvs_ref, kvw_ref, blk_ref, ovl_ref, diag_ref, o_ref, s_buf, e_buf, ovt_ref,
                *, seq):
    dh, qb, rep, groups = NSA_HEAD_DIM, q_ref.shape[0], NSA_REP, NSA_KV_HEADS
    n_cmp_rows = seq // CMP_STRIDE
    n_slc = seq // SEL_BLOCK
    k_sel = min(N_SELECT, n_slc)
    kc = min(SEL_KEY_CHUNK, seq)
    win_keys = WINDOW + qb
    sel_shift = SEL_BLOCK.bit_length() - 1
    gw = rep * dh

    bi = pl.program_id(1)
    t0 = bi * qb

    @pl.when(bi == 0)
    def _():
        chunk_lanes = lax.broadcasted_iota(jnp.int32, (kc, KV_PACK), 1)
        for g in range(groups):
            for c in range(seq // kc):
                ov = jnp.where(chunk_lanes < dh, 1.0, kvs_ref[g, c * kc:(c + 1) * kc, :].astype(F32))
                ovt_ref[g, c] = ov.T.astype(ovt_ref.dtype)

    t_col = t0 + lax.broadcasted_iota(jnp.int32, (qb, 1), 0)
    t_row = t0 + lax.broadcasted_iota(jnp.int32, (1, qb), 1)
    lane = lax.broadcasted_iota(jnp.int32, (qb, KV_PACK), 1)

    prepared = _nsa_prepare(q_ref, cmp_ref, kvs_ref, kvw_ref, ovl_ref, diag_ref, t0, t_col, t_row, lane,
                            seq=seq, qb=qb)
    q_aug = [p[0] for p in prepared]

    def scores(g, c, slot):
        k0 = pl.multiple_of(c * kc, kc)
        keys = jnp.concatenate([kvs_ref[g, pl.ds(k0, kc), :], blk_ref[pl.ds(k0, kc), :]], axis=1)
        s_buf[g, slot] = _dot_nt(keys, q_aug[g])

    def softmax_step(g, slot, m):
        s = s_buf[g, slot]
        m_new = jnp.maximum(m, jnp.max(s, axis=0, keepdims=True))
        e_buf[g, slot] = jnp.exp2(s - m_new).astype(e_buf.dtype)
        return m_new, jnp.exp2(m - m_new)

    def weighted_values(g, c, slot, alpha, acc):
        return alpha * acc + _dot(ovt_ref[g, c], e_buf[g, slot])

    last_chunk = seq // kc - 1

    def pair_body(i, carry):
        c = 2 * i
        heads = range(groups)
        m, acc, alpha_prev = ([carry[g][k] for g in heads] for k in range(3))
        alpha_even, alpha_odd = [None] * groups, [None] * groups
        for g in heads:
            scores(g, c + 1, 1)
        for g in heads:
            m[g], alpha_even[g] = softmax_step(g, 0, m[g])
        for g in heads:
            acc[g] = weighted_values(g, jnp.maximum(c - 1, 0), 1, alpha_prev[g], acc[g])
        for g in heads:
            scores(g, jnp.minimum(c + 2, last_chunk), 0)
        for g in heads:
            m[g], alpha_odd[g] = softmax_step(g, 1, m[g])
        for g in heads:
            acc[g] = weighted_values(g, c, 0, alpha_even[g], acc[g])
        return tuple((m[g], acc[g], alpha_odd[g]) for g in heads)

    n_pairs = ((t0 + qb - SEL_BLOCK + kc - 1) // kc + 1) // 2
    for g in range(groups):
        scores(g, 0, 0)
        e_buf[g, 1] = jnp.zeros(e_buf.shape[2:], e_buf.dtype)
    init = tuple((p[3], p[4], jnp.ones_like(p[3])) for p in prepared)
    final = lax.fori_loop(0, n_pairs, pair_body, init)

    lane4 = lax.broadcasted_iota(jnp.int32, (rep * qb, KV_PACK), 1)
    gates = jax.nn.sigmoid(gl_ref[...])
    for g in range(groups):
        _, o_c, o_w, _, _ = prepared[g]
        _, acc_s, alpha_last = final[g]
        acc_s = weighted_values(g, 2 * n_pairs - 1, 1, alpha_last, acc_s).T
        o_s = acc_s / jnp.where(lane4 < dh, 1.0, pltpu.roll(acc_s, dh, 1))

        outs = []
        for r in range(rep):
            rows = slice(r * qb, (r + 1) * qb)
            gc = 3 * (g * rep + r)
            outs.append(gates[:, gc:gc + 1] * o_c[rows] + gates[:, gc + 1:gc + 2] * o_s[rows]
                        + gates[:, gc + 2:gc + 3] * o_w[rows])
        for p in range(rep // 2):
            pair = jnp.where(lane < dh, pltpu.roll(outs[2 * p], dh, 1), outs[2 * p + 1])
            c0 = g * gw + p * KV_PACK
            o_ref[:, c0:c0 + KV_PACK] = pair.astype(o_ref.dtype)


def _nsa_prepare(q_ref, cmp_ref, kvs_ref, kvw_ref, ovl_ref, diag_ref, t0, t_col, t_row, lane, *, seq, qb):
    dh, rep = NSA_HEAD_DIM, NSA_REP
    heads = range(NSA_KV_HEADS)
    n_cmp_rows = seq // CMP_STRIDE
    n_slc = seq // SEL_BLOCK
    k_sel = min(N_SELECT, n_slc)
    win_keys = WINDOW + qb
    sel_shift = SEL_BLOCK.bit_length() - 1
    gw = rep * dh

    q4 = []
    for g in heads:
        parts = []
        for p in range(rep // 2):
            c0 = g * gw + p * KV_PACK
            pair = q_ref[:, c0:c0 + KV_PACK].astype(F32) * (dh ** -0.5 * LOG2E)
            parts.append(jnp.where(lane < dh, pair, 0.0))
            parts.append(jnp.where(lane < dh, pltpu.roll(pair, dh, 1), 0.0))
        q4.append(jnp.concatenate(parts, axis=0).astype(MXU_DTYPE))

    n_idx = lax.broadcasted_iota(jnp.int32, (1, n_cmp_rows), 1)
    valid_c = (n_idx * CMP_STRIDE + (CMP_BLOCK - 1) <= t_col) & (n_idx < n_cmp_rows - 1)
    bias_c = jnp.where(valid_c, 0.0, NEG_INF)
    o_c, p_sum = [], []
    for g in heads:
        kvc = cmp_ref[g]
        e_c, inv_c = _softmax2_parts(_dot_nt(q4[g], kvc).reshape(rep, qb, n_cmp_rows) + bias_c[None])
        p_c = e_c * inv_c
        o_c.append(_dot(p_c.reshape(rep * qb, n_cmp_rows), kvc))
        total = p_c[0]
        for r in range(1, rep):
            total = total + p_c[r]
        p_sum.append(total)

    jj = lax.broadcasted_iota(jnp.int32, (n_slc, qb), 0)
    cur = t_row >> sel_shift
    forced = (jj == 0) | (jj == cur) | (jj == cur - 1)
    causal = jj <= cur
    jf = jj.astype(F32)
    imp = []
    for g in heads:
        acc = None
        for term in _split3(p_sum[g]):
            d = _dot_nt(ovl_ref[...], term)
            acc = d if acc is None else acc + d
        imp.append(jnp.where(causal, jnp.where(forced, FORCE, acc), -FORCE))
    sel = [jnp.zeros((n_slc, qb), F32) for _ in heads]
    for _ in range(k_sel):
        for g in heads:
            mx = jnp.max(imp[g], axis=0, keepdims=True)
            first = jnp.min(jnp.where(imp[g] == mx, jf, float(n_slc)), axis=0, keepdims=True)
            pick = jf == first
            sel[g] = jnp.where(pick, 1.0, sel[g])
            imp[g] = jnp.where(pick, KNOCKOUT, imp[g])
    before_own = jj < cur
    q_aug = []
    for g in heads:
        bias_t = jnp.where((sel[g] > 0.5) & before_own, 0.0, NEG_INF)
        neg_sel = bias_t.T.astype(MXU_DTYPE)
        q_aug.append(jnp.concatenate([q4[g], jnp.concatenate([neg_sel] * rep, axis=0)], axis=1))

    w0 = pl.multiple_of(jnp.maximum(t0 - WINDOW, 0), qb)
    kp = w0 + lax.broadcasted_iota(jnp.int32, (1, win_keys), 1)
    bias_w = jnp.where((kp <= t_col) & (kp > t_col - WINDOW), 0.0, NEG_INF)
    o_w = []
    for g in heads:
        kvw = kvw_ref[g, pl.ds(w0, win_keys), :]
        e_w, inv_w = _softmax2_parts(_dot_nt(q4[g], kvw).reshape(rep, qb, win_keys) + bias_w[None])
        o_w.append(_dot(e_w.reshape(rep * qb, win_keys), kvw) * inv_w.reshape(rep * qb, 1))

    m0, acc0 = [], []
    for g in heads:
        kv_own = kvs_ref[g, pl.ds(pl.multiple_of(t0, qb), qb), :]
        ov_own_t = jnp.where(lane < dh, 1.0, kv_own.astype(F32)).T
        s_d = _dot_nt(kv_own, q4[g]) + diag_ref[...]
        m0.append(jnp.max(s_d, axis=0, keepdims=True))
        acc0.append(_dot(ov_own_t, jnp.exp2(s_d - m0[g])))
    return [(q_aug[g], o_c[g], o_w[g], m0[g], acc0[g]) for g in heads]


def _nsa(q, gl, kv_cmp, kv6):
    b, s, _ = q.shape
    qt = min(NSA_Q_TILE, s)
    nb = s // qt
    gw = NSA_REP * NSA_HEAD_DIM
    n_cmp_rows = s // CMP_STRIDE
    n_slc = s // SEL_BLOCK
    kc = min(SEL_KEY_CHUNK, s)
    assert (s // kc) % 2 == 0, "the chunk loop walks key chunks in pairs"
    key_block = jnp.arange(s, dtype=jnp.int32)[:, None] // SEL_BLOCK
    blk_onehot = (key_block == jnp.arange(n_slc, dtype=jnp.int32)[None, :]).astype(MXU_DTYPE)
    sj = jnp.arange(n_slc, dtype=jnp.int32)[:, None] * SEL_BLOCK
    ci = jnp.arange(n_cmp_rows, dtype=jnp.int32)[None, :] * CMP_STRIDE
    overlap_t = ((ci < sj + SEL_BLOCK) & (ci + CMP_BLOCK > sj)).astype(MXU_DTYPE)
    ti = jnp.arange(qt, dtype=jnp.int32)
    own_block = (ti[:, None] // SEL_BLOCK == ti[None, :] // SEL_BLOCK) & (ti[:, None] <= ti[None, :])
    diag_bias = jnp.tile(jnp.where(own_block, 0.0, NEG_INF).astype(F32), (1, NSA_REP))
    ng = NSA_KV_HEADS
    const = lambda shape: pl.BlockSpec(shape, lambda bb, i: (0,) * len(shape), pipeline_mode=pl.Buffered(1))
    return pl.pallas_call(
        functools.partial(_nsa_kernel, seq=s),
        grid=(b, nb),
        in_specs=[
            pl.BlockSpec((None, qt, ng * gw), lambda bb, i: (bb, i, 0)),
            pl.BlockSpec((None, qt, GATE_PAD), lambda bb, i: (bb, i, 0)),
            pl.BlockSpec((None, ng, n_cmp_rows, KV_PACK), lambda bb, i: (bb, 0, 0, 0)),
            pl.BlockSpec((None, ng, s, KV_PACK), lambda bb, i: (bb, 1, 0, 0)),
            pl.BlockSpec((None, ng, s, KV_PACK), lambda bb, i: (bb, 2, 0, 0)),
            const((s, n_slc)), const((n_slc, n_cmp_rows)), const((qt, NSA_REP * qt)),
        ],
        out_specs=pl.BlockSpec((None, qt, ng * gw), lambda bb, i: (bb, i, 0)),
        out_shape=jax.ShapeDtypeStruct((b, s, NSA_WIDTH), MXU_DTYPE),
        scratch_shapes=[pltpu.VMEM((ng, 2, kc, NSA_REP * qt), F32),
                        pltpu.VMEM((ng, 2, kc, NSA_REP * qt), MXU_DTYPE),
                        pltpu.VMEM((ng, s // kc, KV_PACK, kc), MXU_DTYPE)],
        compiler_params=_params("parallel", "arbitrary"),
        name="nsa_attention",
    )(q, gl, kv_cmp, kv6, kv6, blk_onehot, overlap_t, diag_bias)


def _mix_out_kernel(x_ref, nsa_ref, zg_ref, zc_ref, zch_ref, gln_g_ref, gln_b_ref, ws_ref, bs_ref,
                    dww_ref, dwb_ref, cln_g_ref, cln_b_ref, og_ref, wo_ref, o_ref, hbuf_ref):
    tm = x_ref.shape[0]
    i = pl.program_id(1)

    def glu(z):
        z = z.astype(F32)
        return z[:, :CONV_WIDTH] * jax.nn.sigmoid(z[:, CONV_WIDTH:])

    halo = glu(zch_ref[...])
    hbuf_ref[0, 0:CONV_HALO, :] = jnp.where(i == 0, 0.0, halo)
    hbuf_ref[0, CONV_HALO:CONV_HALO + tm, :] = glu(zc_ref[...])
    n_shifted = CONV_HALO + tm - F32_SUBLANES
    for c in range(1, F32_SUBLANES):
        hbuf_ref[c, 0:n_shifted, :] = hbuf_ref[0, c:c + n_shifted, :]
    conv = jnp.broadcast_to(dwb_ref[...], (tm, CONV_WIDTH))
    for k in range(CONV_TAPS):
        off = CONV_HALO - (CONV_TAPS - 1) + k
        c = off % F32_SUBLANES
        conv = conv + dww_ref[k:k + 1, :] * hbuf_ref[c, off - c:off - c + tm, :]
    y_conv = jax.nn.silu(_layer_norm(conv, cln_g_ref[...], cln_b_ref[...]))

    zg = _gelu_tanh(zg_ref[...].astype(F32))
    u = zg[:, :GMLP_WIDTH]
    v = _layer_norm(zg[:, GMLP_WIDTH:], gln_g_ref[...], gln_b_ref[...]).astype(MXU_DTYPE)
    row = lax.broadcasted_iota(jnp.int32, (GMLP_CHUNK, GMLP_CHUNK), 0)
    col = lax.broadcasted_iota(jnp.int32, (GMLP_CHUNK, GMLP_CHUNK), 1)
    grp = lax.broadcasted_iota(jnp.int32, (GMLP_CHUNK, GMLP_WIDTH), 1) // (GMLP_WIDTH // GMLP_GROUPS)
    w_causal = [jnp.where(row >= col, ws_ref[g], 0.0).astype(MXU_DTYPE) for g in range(GMLP_GROUPS)]
    bias = jnp.zeros((GMLP_CHUNK, GMLP_WIDTH), F32)
    for g in range(GMLP_GROUPS):
        bias = jnp.where(grp == g, bs_ref[:, g:g + 1], bias)
    mixed = []
    for c in range(tm // GMLP_CHUNK):
        vc = v[c * GMLP_CHUNK:(c + 1) * GMLP_CHUNK]
        acc = bias
        for g in range(GMLP_GROUPS):
            acc = acc + jnp.where(grp == g, jnp.dot(w_causal[g], vc, preferred_element_type=F32), 0.0)
        mixed.append(acc)
    y_gmlp = u * jnp.concatenate(mixed, axis=0)

    og = og_ref[...]
    n0, n1 = NSA_WIDTH, NSA_WIDTH + GMLP_WIDTH
    y = _dot(_rms(nsa_ref[...].astype(F32), og[:, :n0]), wo_ref[:n0, :])
    y = y + _dot(_rms(y_gmlp, og[:, n0:n1]), wo_ref[n0:n1, :])
    y = y + _dot(_rms(y_conv, og[:, n1:]), wo_ref[n1:, :])
    o_ref[...] = x_ref[...] + y


def _mix_out(x, y_nsa, zg, zc, p):
    b, s, d = x.shape
    tm = min(MIX_ROWS, s)
    halo_blocks = tm // CONV_HALO
    row2 = lambda a: a.reshape(1, -1)
    full = lambda shape: pl.BlockSpec(shape, lambda bb, i: (0,) * len(shape))
    tile = lambda w: pl.BlockSpec((None, tm, w), lambda bb, i: (bb, i, 0))
    return pl.pallas_call(
        _mix_out_kernel,
        grid=(b, s // tm),
        in_specs=[
            tile(d), tile(NSA_WIDTH), tile(2 * GMLP_WIDTH), tile(2 * CONV_WIDTH),
            pl.BlockSpec((None, CONV_HALO, 2 * CONV_WIDTH),
                         lambda bb, i: (bb, jnp.maximum(i * halo_blocks - 1, 0), 0)),
            full((1, GMLP_WIDTH)), full((1, GMLP_WIDTH)),
            full((GMLP_GROUPS, GMLP_CHUNK, GMLP_CHUNK)), full((GMLP_CHUNK, GMLP_GROUPS)),
            full((CONV_TAPS, CONV_WIDTH)), full((1, CONV_WIDTH)), full((1, CONV_WIDTH)), full((1, CONV_WIDTH)),
            full((1, d)), full((d, d)),
        ],
        out_specs=tile(d),
        out_shape=jax.ShapeDtypeStruct((b, s, d), F32),
        scratch_shapes=[pltpu.VMEM((F32_SUBLANES, CONV_HALO + tm, CONV_WIDTH), F32)],
        compiler_params=_params("parallel", "parallel"),
        name="mix_out",
    )(x, y_nsa, zg, zc, zc, row2(p["gmlp_ln_g"]), row2(p["gmlp_ln_b"]), p["gmlp_ws"], p["gmlp_bs"].T,
      p["conv_dw_w"], row2(p["conv_dw_b"]), row2(p["conv_ln_g"]), row2(p["conv_ln_b"]),
      row2(p["mix_out_g"]), p["w_out"].astype(MXU_DTYPE))


def _mem_kv_kernel(m_ref, g_ref, wk_ref, wv_ref, k_ref, v_ref):
    m = _rms(m_ref[...], g_ref[...]).astype(MXU_DTYPE)
    k_ref[...] = _dot(m, wk_ref[...]).astype(k_ref.dtype)
    v_ref[...] = _dot(m, wv_ref[...]).astype(v_ref.dtype)


def _mem_kv(mem, g, wk, wv):
    b, n, d = mem.shape
    blk = pl.BlockSpec((None, n, d), lambda bb: (bb, 0, 0))
    wspec = pl.BlockSpec((d, d), lambda bb: (0, 0))
    return pl.pallas_call(
        _mem_kv_kernel,
        grid=(b,),
        in_specs=[blk, pl.BlockSpec((1, d), lambda bb: (0, 0)), wspec, wspec],
        out_specs=[blk, blk],
        out_shape=[jax.ShapeDtypeStruct((b, n, d), MXU_DTYPE)] * 2,
        compiler_params=_params("parallel"),
        name="xattn_mem_kv",
    )(mem, g.reshape(1, d), wk.astype(MXU_DTYPE), wv.astype(MXU_DTYPE))


def _xattn_kernel(x_ref, g_ref, wq_ref, k_ref, v_ref, wo_ref, o_ref, att_ref):
    dh = XATTN_HEAD_DIM
    heads = [slice(hd * dh, (hd + 1) * dh) for hd in range(XATTN_HEADS)]
    x = x_ref[...]
    h = _rms(x, g_ref[...]).astype(MXU_DTYPE)
    q = (_dot(h, wq_ref[...]) * (dh ** -0.5 * LOG2E)).astype(MXU_DTYPE)
    s = [_dot_nt(q[:, cols], k_ref[:, cols]) for cols in heads]
    e = [jnp.exp2(sh - jnp.max(sh, axis=-1, keepdims=True)) for sh in s]
    inv = [1.0 / jnp.sum(eh, axis=-1, keepdims=True) for eh in e]
    for cols, eh, ih in zip(heads, e, inv):
        att_ref[:, cols] = (_dot(eh, v_ref[:, cols]) * ih).astype(att_ref.dtype)
    o_ref[...] = x + _dot(att_ref[...], wo_ref[...])


def _xattn(x, g, wq, k, v, wo):
    b, s, d = x.shape
    n = k.shape[1]
    tm = min(XATTN_ROWS, s)
    tile = pl.BlockSpec((None, tm, d), lambda bb, i: (bb, i, 0))
    wspec = pl.BlockSpec((d, d), lambda bb, i: (0, 0))
    mspec = pl.BlockSpec((None, n, d), lambda bb, i: (bb, 0, 0))
    return pl.pallas_call(
        _xattn_kernel,
        grid=(b, s // tm),
        in_specs=[tile, pl.BlockSpec((1, d), lambda bb, i: (0, 0)), wspec, mspec, mspec, wspec],
        out_specs=tile,
        out_shape=jax.ShapeDtypeStruct((b, s, d), F32),
        scratch_shapes=[pltpu.VMEM((tm, d), MXU_DTYPE)],
        compiler_params=_params("parallel", "parallel"),
        name="xattn",
    )(x, g.reshape(1, d), wq.astype(MXU_DTYPE), k, v, wo.astype(MXU_DTYPE))


def _ffn_kernel(x_ref, xh_ref, g_ref, wup_ref, cw_ref, cb_ref, wd_ref, fg_ref, o_ref, xn_ref, h_buf, act_buf,
                *, final_norm):
    tm = x_ref.shape[0]
    tc = FFN_COLS
    i = pl.program_id(1)
    halo = jnp.where(i == 0, 0.0, xh_ref[...])
    xn_ref[0:FFN_HALO, :] = _rms(halo, g_ref[...]).astype(xn_ref.dtype)
    xn_ref[FFN_HALO:FFN_HALO + tm, :] = _rms(x_ref[...], g_ref[...]).astype(xn_ref.dtype)
    xn = xn_ref[...]

    def conv(slot, half, c0):
        cols = slice(c0, c0 + tc)
        y = cb_ref[:, cols]
        for k in range(FFN_CONV_TAPS):
            shift = FFN_CONV_TAPS - 1 - k
            y = y + cw_ref[k:k + 1, cols] * h_buf[slot, half, FFN_HALO - shift:FFN_HALO - shift + tm, :]
        return y

    for j in range(D_FF // tc):
        slot = j % 2
        gate_c0, up_c0 = j * tc, D_FF + j * tc
        h_buf[slot, 0] = _dot(xn, wup_ref[:, gate_c0:gate_c0 + tc])
        h_buf[slot, 1] = _dot(xn, wup_ref[:, up_c0:up_c0 + tc])
        act = jax.nn.silu(conv(slot, 0, gate_c0)) * conv(slot, 1, up_c0)
        act_buf[:, j * tc:(j + 1) * tc] = act.astype(act_buf.dtype)

    y = x_ref[...] + _dot(act_buf[...], wd_ref[...])
    o_ref[...] = _rms(y, fg_ref[...]) if final_norm else y


def _ffn(x, g, w_up, dw_w, dw_b, w_down, final_g, final_norm):
    b, s, d = x.shape
    tm = min(FFN_ROWS, s)
    halo_blocks = tm // FFN_HALO
    tile = pl.BlockSpec((None, tm, d), lambda bb, i: (bb, i, 0))
    resident = lambda shape: pl.BlockSpec(shape, lambda bb, i: (0,) * len(shape), pipeline_mode=pl.Buffered(1))
    return pl.pallas_call(
        functools.partial(_ffn_kernel, final_norm=final_norm),
        grid=(b, s // tm),
        in_specs=[
            tile,
            pl.BlockSpec((None, FFN_HALO, d), lambda bb, i: (bb, jnp.maximum(i * halo_blocks - 1, 0), 0)),
            resident((1, d)),
            resident((d, 2 * D_FF)),
            resident((FFN_CONV_TAPS, 2 * D_FF)),
            resident((1, 2 * D_FF)),
            resident((D_FF, d)),
            resident((1, d)),
        ],
        out_specs=tile,
        out_shape=jax.ShapeDtypeStruct((b, s, d), F32),
        scratch_shapes=[pltpu.VMEM((FFN_HALO + tm, d), MXU_DTYPE),
                        pltpu.VMEM((2, 2, FFN_HALO + tm, FFN_COLS), F32),
                        pltpu.VMEM((tm, D_FF), MXU_DTYPE)],
        compiler_params=_params("parallel", "parallel"),
        name="conv_ffn",
    )(x, x, g.reshape(1, d), w_up.astype(MXU_DTYPE), dw_w, dw_b.reshape(1, -1), w_down.astype(MXU_DTYPE),
      final_g.reshape(1, d))


def kernel(x, mem, mix_norm_g, w_in, cmp_pe, cmp_w1, cmp_w2, gmlp_ln_g, gmlp_ln_b, gmlp_ws, gmlp_bs, conv_dw_w, conv_dw_b, conv_ln_g, conv_ln_b, mix_out_g, w_out, xattn_norm_g, mem_norm_g, xattn_wq, xattn_wk, xattn_wv, xattn_wo, ffn_norm_g, ffn_w_up, ffn_dw_w, ffn_dw_b, ffn_w_down, final_norm_g):
    depth = w_in.shape[0]
    for l in range(depth):
        q, kv6, zg, zc, gl = _in_proj(x, mix_norm_g[l], _permute_w_in(w_in[l]))
        kv_cmp = _compress(kv6, *_pack_cmp_weights(cmp_pe[l], cmp_w1[l], cmp_w2[l]))
        y_nsa = _nsa(q, gl, kv_cmp, kv6)
        mix = dict(gmlp_ln_g=gmlp_ln_g[l], gmlp_ln_b=gmlp_ln_b[l], gmlp_ws=gmlp_ws[l], gmlp_bs=gmlp_bs[l],
                   conv_dw_w=conv_dw_w[l], conv_dw_b=conv_dw_b[l], conv_ln_g=conv_ln_g[l],
                   conv_ln_b=conv_ln_b[l], mix_out_g=mix_out_g[l], w_out=w_out[l])
        x = _mix_out(x, y_nsa, zg, zc, mix)
        mk, mv = _mem_kv(mem, mem_norm_g[l], xattn_wk[l], xattn_wv[l])
        x = _xattn(x, xattn_norm_g[l], xattn_wq[l], mk, mv, xattn_wo[l])
        x = _ffn(x, ffn_norm_g[l], ffn_w_up[l], ffn_dw_w[l], ffn_dw_b[l], ffn_w_down[l], final_norm_g,
                 final_norm=(l == depth - 1))
    return x
```

```python
import functools

import jax
import jax.numpy as jnp
from jax import lax
from jax.experimental import pallas as pl
from jax.experimental.pallas import tpu as pltpu

D_MODEL = 1024
NSA_HEADS = 8
NSA_KV_HEADS = 2
NSA_HEAD_DIM = 64
NSA_REP = NSA_HEADS // NSA_KV_HEADS
CMP_BLOCK = 32
CMP_STRIDE = 16
SEL_BLOCK = 64
N_SELECT = 16
WINDOW = 512
Q_BLOCK = 128
NSA_WIDTH = NSA_HEADS * NSA_HEAD_DIM
KV_WIDTH = NSA_KV_HEADS * NSA_HEAD_DIM
GMLP_WIDTH = 256
GMLP_GROUPS = 4
GMLP_CHUNK = 128
CONV_WIDTH = 256
CONV_TAPS = 31
XATTN_HEADS = 4
XATTN_HEAD_DIM = D_MODEL // XATTN_HEADS
D_FF = 2816
FFN_CONV_TAPS = 3
EPS = 1e-6
NEG_INF = -1e30
TINY = 1e-30
FORCE = 1e4
LOG2E = 1.4426950408889634
MAX_FLOOR = -1e29
KNOCKOUT = -3.0e38
N_FORCED = 3

LANES = 128
F32_SUBLANES = 8
BF16_SUBLANES = 16
VMEM_LIMIT_BYTES = 48 * 1024 * 1024

MXU_DTYPE = jnp.bfloat16
F32 = jnp.float32

IN_PROJ_ROWS = 512
NSA_Q_TILE = 128
SEL_KEY_CHUNK = 512
MIX_ROWS = 512
CONV_HALO = 32
XATTN_ROWS = 512
FFN_ROWS = 512
FFN_COLS = 256
FFN_HALO = 16
GATE_PAD = LANES
KV_PACK = 2 * NSA_HEAD_DIM


def _params(*semantics):
    return pltpu.CompilerParams(dimension_semantics=semantics, vmem_limit_bytes=VMEM_LIMIT_BYTES)


def _dot(a, b):
    return jnp.dot(a.astype(MXU_DTYPE), b.astype(MXU_DTYPE), preferred_element_type=F32)


def _dot_nt(a, b):
    return lax.dot_general(a.astype(MXU_DTYPE), b.astype(MXU_DTYPE), (((1,), (1,)), ((), ())),
                           preferred_element_type=F32)


def _rms(x, g):
    return x * lax.rsqrt(jnp.mean(x * x, axis=-1, keepdims=True) + EPS) * g


def _layer_norm(x, g, b):
    mu = jnp.mean(x, axis=-1, keepdims=True)
    xc = x - mu
    var = jnp.mean(xc * xc, axis=-1, keepdims=True)
    return xc * lax.rsqrt(var + EPS) * g + b


def _gelu_tanh(x):
    return 0.5 * x * (1.0 + jnp.tanh(0.7978845608028654 * (x + 0.044715 * (x * x * x))))


def _masked_softmax(s, valid):
    s = jnp.where(valid, s, NEG_INF)
    m = jnp.max(s, axis=-1, keepdims=True)
    e = jnp.where(valid, jnp.exp(s - m), 0.0)
    return e / jnp.maximum(jnp.sum(e, axis=-1, keepdims=True), TINY)


IN_Q = (0, NSA_WIDTH)
IN_KV = (IN_Q[1], IN_Q[1] + 6 * KV_PACK)
IN_ZG = (IN_KV[1], IN_KV[1] + 2 * GMLP_WIDTH)
IN_ZC = (IN_ZG[1], IN_ZG[1] + 2 * CONV_WIDTH)
IN_GL = (IN_ZC[1], IN_ZC[1] + GATE_PAD)
IN_COLS = IN_GL[1]


def _layer_spec(stacked, layer, **kwargs):
    shape = stacked.shape[1:]
    return pl.BlockSpec((None,) + shape, lambda *_: (layer,) + (0,) * len(shape), **kwargs)


def _permute_w_in(w):
    off_kv = NSA_WIDTH
    off_gl = NSA_WIDTH + 6 * KV_WIDTH
    off_zg = off_gl + NSA_HEADS * 3
    off_zc = off_zg + 2 * GMLP_WIDTH
    parts = [w[..., :NSA_WIDTH]]
    for branch in range(3):
        k0 = off_kv + (2 * branch) * KV_WIDTH
        v0 = off_kv + (2 * branch + 1) * KV_WIDTH
        for g in range(NSA_KV_HEADS):
            parts.append(w[..., k0 + g * NSA_HEAD_DIM:k0 + (g + 1) * NSA_HEAD_DIM])
            parts.append(w[..., v0 + g * NSA_HEAD_DIM:v0 + (g + 1) * NSA_HEAD_DIM])
    parts.append(w[..., off_zg:off_zg + 2 * GMLP_WIDTH])
    parts.append(w[..., off_zc:off_zc + 2 * CONV_WIDTH])
    n_gate = NSA_HEADS * 3
    parts.append(jnp.pad(w[..., off_gl:off_gl + n_gate], ((0, 0), (0, 0), (0, GATE_PAD - n_gate))))
    return jnp.concatenate(parts, axis=-1).astype(MXU_DTYPE)


def _in_proj_kernel(x_ref, g_ref, w_ref, q_ref, kv_ref, zg_ref, zc_ref, gl_ref):
    h = _rms(x_ref[...], g_ref[...]).astype(MXU_DTYPE)
    q_ref[...] = _dot(h, w_ref[:, IN_Q[0]:IN_Q[1]]).astype(q_ref.dtype)
    for j in range(6):
        c0 = IN_KV[0] + j * KV_PACK
        kv_ref[j] = _dot(h, w_ref[:, c0:c0 + KV_PACK]).astype(kv_ref.dtype)
    zg_ref[...] = _dot(h, w_ref[:, IN_ZG[0]:IN_ZG[1]]).astype(zg_ref.dtype)
    zc_ref[...] = _dot(h, w_ref[:, IN_ZC[0]:IN_ZC[1]]).astype(zc_ref.dtype)
    gl_ref[...] = _dot(h, w_ref[:, IN_GL[0]:IN_GL[1]])


def _in_proj(x, g, w, layer):
    b, s, d = x.shape
    tm = IN_PROJ_ROWS
    return pl.pallas_call(
        _in_proj_kernel,
        grid=(b, s // tm),
        in_specs=[
            pl.BlockSpec((None, tm, d), lambda bb, i: (bb, i, 0)),
            pl.BlockSpec((1, d), lambda bb, i: (0, 0)),
            _layer_spec(w, layer),
        ],
        out_specs=[
            pl.BlockSpec((None, tm, NSA_WIDTH), lambda bb, i: (bb, i, 0)),
            pl.BlockSpec((None, 6, tm, KV_PACK), lambda bb, i: (bb, 0, i, 0)),
            pl.BlockSpec((None, tm, 2 * GMLP_WIDTH), lambda bb, i: (bb, i, 0)),
            pl.BlockSpec((None, tm, 2 * CONV_WIDTH), lambda bb, i: (bb, i, 0)),
            pl.BlockSpec((None, tm, GATE_PAD), lambda bb, i: (bb, i, 0)),
        ],
        out_shape=[
            jax.ShapeDtypeStruct((b, s, NSA_WIDTH), MXU_DTYPE),
            jax.ShapeDtypeStruct((b, 6, s, KV_PACK), MXU_DTYPE),
            jax.ShapeDtypeStruct((b, s, 2 * GMLP_WIDTH), MXU_DTYPE),
            jax.ShapeDtypeStruct((b, s, 2 * CONV_WIDTH), MXU_DTYPE),
            jax.ShapeDtypeStruct((b, s, GATE_PAD), F32),
        ],
        compiler_params=_params("parallel", "parallel"),
        name="in_proj",
    )(x, g.reshape(1, d), w)


CMP_ROW = CMP_STRIDE * KV_PACK


def _pack_cmp_weights(pe, w1, w2):
    dh = NSA_HEAD_DIM
    w1r = w1.reshape(2, CMP_BLOCK, dh, dh)
    z = jnp.zeros_like(w1r[0])
    wk = jnp.concatenate([w1r[0], z], axis=-1)
    wv = jnp.concatenate([z, w1r[1]], axis=-1)
    w1p = jnp.concatenate([wk, wv], axis=1).reshape(CMP_BLOCK * KV_PACK, KV_PACK)
    z2 = jnp.zeros_like(w2[0])
    w2p = jnp.concatenate([jnp.concatenate([w2[0], z2], axis=1), jnp.concatenate([z2, w2[1]], axis=1)], axis=0)
    pep = jnp.concatenate([pe[0], pe[1]], axis=-1).reshape(1, CMP_BLOCK * KV_PACK)
    pep = jnp.broadcast_to(pep, (8, CMP_BLOCK * KV_PACK))
    return pep.astype(MXU_DTYPE), w1p.astype(MXU_DTYPE), w2p.astype(MXU_DTYPE)


def _compress_kernel(c_ref, pe_ref, w1_ref, w2_ref, o_ref):
    c = c_ref[...]
    n = c.shape[0]
    first = _dot(c, w1_ref[:CMP_ROW, :])
    second = _dot(c, w1_ref[CMP_ROW:, :])
    pe_term = _dot(pe_ref[...], w1_ref[...])[0:1, :]
    h = first + pltpu.roll(second, n - 1, 0) + pe_term
    o_ref[...] = _dot(_gelu_tanh(h), w2_ref[...]).astype(o_ref.dtype)


def _compress(kv6, pep, w1p, w2p):
    b, _, s, _ = kv6.shape
    n = s // CMP_STRIDE
    c = kv6[:, :NSA_KV_HEADS].reshape(b, NSA_KV_HEADS, n, CMP_ROW)
    return pl.pallas_call(
        _compress_kernel,
        grid=(b, NSA_KV_HEADS),
        in_specs=[
            pl.BlockSpec((None, None, n, CMP_ROW), lambda bb, g: (bb, g, 0, 0)),
            pl.BlockSpec((8, 2 * CMP_ROW), lambda bb, g: (0, 0)),
            pl.BlockSpec((2 * CMP_ROW, KV_PACK), lambda bb, g: (0, 0)),
            pl.BlockSpec((KV_PACK, KV_PACK), lambda bb, g: (0, 0)),
        ],
        out_specs=pl.BlockSpec((None, None, n, KV_PACK), lambda bb, g: (bb, g, 0, 0)),
        out_shape=jax.ShapeDtypeStruct((b, NSA_KV_HEADS, n, KV_PACK), MXU_DTYPE),
        compiler_params=_params("parallel", "parallel"),
        name="nsa_compress",
    )(c, pep, w1p, w2p)


def _split3(x):
    a = x.astype(MXU_DTYPE)
    r = x - a.astype(F32)
    b = r.astype(MXU_DTYPE)
    c = (r - b.astype(F32)).astype(MXU_DTYPE)
    return a, b, c


def _softmax2_parts(s):
    m = jnp.maximum(jnp.max(s, axis=-1, keepdims=True), MAX_FLOOR)
    e = jnp.exp2(s - m)
    return e, 1.0 / jnp.maximum(jnp.sum(e, axis=-1, keepdims=True), TINY)


def _nsa_kernel(q_ref, gl_ref, cmp_ref, kvs_ref, kvw_ref, blk_ref, ovl_ref, diag_ref, o_ref, s_buf, e_buf, ovt_ref,
                *, seq):
    dh, qb, rep, groups = NSA_HEAD_DIM, q_ref.shape[0], NSA_REP, NSA_KV_HEADS
    n_cmp_rows = seq // CMP_STRIDE
    n_slc = seq // SEL_BLOCK
    k_sel = min(N_SELECT, n_slc)
    kc = min(SEL_KEY_CHUNK, seq)
    win_keys = WINDOW + qb
    sel_shift = SEL_BLOCK.bit_length() - 1
    gw = rep * dh

    bi = pl.program_id(1)
    t0 = bi * qb

    @pl.when(bi == 0)
    def _():
        chunk_lanes = lax.broadcasted_iota(jnp.int32, (kc, KV_PACK), 1)
        for g in range(groups):
            for c in range(seq // kc):
                ov = jnp.where(chunk_lanes < dh, 1.0, kvs_ref[g, c * kc:(c + 1) * kc, :].astype(F32))
                ovt_ref[g, c] = ov.T.astype(ovt_ref.dtype)

    t_col = t0 + lax.broadcasted_iota(jnp.int32, (qb, 1), 0)
    t_row = t0 + lax.broadcasted_iota(jnp.int32, (1, qb), 1)
    lane = lax.broadcasted_iota(jnp.int32, (qb, KV_PACK), 1)

    prepared = _nsa_prepare(q_ref, cmp_ref, kvs_ref, kvw_ref, ovl_ref, diag_ref, t0, t_col, t_row, lane,
                            seq=seq, qb=qb)
    q_aug = [p[0] for p in prepared]

    def scores(g, c, slot):
        k0 = pl.multiple_of(c * kc, kc)
        keys = jnp.concatenate([kvs_ref[g, pl.ds(k0, kc), :], blk_ref[pl.ds(k0, kc), :]], axis=1)
        s_buf[g, slot] = _dot_nt(keys, q_aug[g])

    def softmax_step(g, slot, m):
        s = s_buf[g, slot]
        m_new = jnp.maximum(m, jnp.max(s, axis=0, keepdims=True))
        e_buf[g, slot] = jnp.exp2(s - m_new).astype(e_buf.dtype)
        return m_new, jnp.exp2(m - m_new)

    def weighted_values(g, c, slot, alpha, acc):
        return alpha * acc + _dot(ovt_ref[g, c], e_buf[g, slot])

    last_chunk = seq // kc - 1

    def pair_body(i, carry):
        c = 2 * i
        heads = range(groups)
        m, acc, alpha_prev = ([carry[g][k] for g in heads] for k in range(3))
        alpha_even, alpha_odd = [None] * groups, [None] * groups
        for g in heads:
            scores(g, c + 1, 1)
        for g in heads:
            m[g], alpha_even[g] = softmax_step(g, 0, m[g])
        for g in heads:
            acc[g] = weighted_values(g, jnp.maximum(c - 1, 0), 1, alpha_prev[g], acc[g])
        for g in heads:
            scores(g, jnp.minimum(c + 2, last_chunk), 0)
        for g in heads:
            m[g], alpha_odd[g] = softmax_step(g, 1, m[g])
        for g in heads:
            acc[g] = weighted_values(g, c, 0, alpha_even[g], acc[g])
        return tuple((m[g], acc[g], alpha_odd[g]) for g in heads)

    n_pairs = ((t0 + qb - SEL_BLOCK + kc - 1) // kc + 1) // 2
    for g in range(groups):
        scores(g, 0, 0)
        e_buf[g, 1] = jnp.zeros(e_buf.shape[2:], e_buf.dtype)
    init = tuple((p[3], p[4], jnp.ones_like(p[3])) for p in prepared)
    final = lax.fori_loop(0, n_pairs, pair_body, init)

    lane4 = lax.broadcasted_iota(jnp.int32, (rep * qb, KV_PACK), 1)
    gates = jax.nn.sigmoid(gl_ref[...])
    for g in range(groups):
        _, o_c, o_w, _, _ = prepared[g]
        _, acc_s, alpha_last = final[g]
        acc_s = weighted_values(g, 2 * n_pairs - 1, 1, alpha_last, acc_s).T
        o_s = acc_s / jnp.where(lane4 < dh, 1.0, pltpu.roll(acc_s, dh, 1))

        outs = []
        for r in range(rep):
            rows = slice(r * qb, (r + 1) * qb)
            gc = 3 * (g * rep + r)
            outs.append(gates[:, gc:gc + 1] * o_c[rows] + gates[:, gc + 1:gc + 2] * o_s[rows]
                        + gates[:, gc + 2:gc + 3] * o_w[rows])
        for p in range(rep // 2):
            pair = jnp.where(lane < dh, pltpu.roll(outs[2 * p], dh, 1), outs[2 * p + 1])
            c0 = g * gw + p * KV_PACK
            o_ref[:, c0:c0 + KV_PACK] = pair.astype(o_ref.dtype)


def _nsa_prepare(q_ref, cmp_ref, kvs_ref, kvw_ref, ovl_ref, diag_ref, t0, t_col, t_row, lane, *, seq, qb):
    dh, rep = NSA_HEAD_DIM, NSA_REP
    heads = range(NSA_KV_HEADS)
    n_cmp_rows = seq // CMP_STRIDE
    n_slc = seq // SEL_BLOCK
    k_sel = min(N_SELECT, n_slc)
    win_keys = WINDOW + qb
    sel_shift = SEL_BLOCK.bit_length() - 1
    gw = rep * dh

    q4 = []
    for g in heads:
        parts = []
        for p in range(rep // 2):
            c0 = g * gw + p * KV_PACK
            pair = q_ref[:, c0:c0 + KV_PACK].astype(F32) * (dh ** -0.5 * LOG2E)
            parts.append(jnp.where(lane < dh, pair, 0.0))
            parts.append(jnp.where(lane < dh, pltpu.roll(pair, dh, 1), 0.0))
        q4.append(jnp.concatenate(parts, axis=0).astype(MXU_DTYPE))

    n_idx = lax.broadcasted_iota(jnp.int32, (1, n_cmp_rows), 1)
    valid_c = (n_idx * CMP_STRIDE + (CMP_BLOCK - 1) <= t_col) & (n_idx < n_cmp_rows - 1)
    bias_c = jnp.where(valid_c, 0.0, NEG_INF)
    o_c, p_sum = [], []
    for g in heads:
        kvc = cmp_ref[g]
        e_c, inv_c = _softmax2_parts(_dot_nt(q4[g], kvc).reshape(rep, qb, n_cmp_rows) + bias_c[None])
        p_c = e_c * inv_c
        o_c.append(_dot(p_c.reshape(rep * qb, n_cmp_rows), kvc))
        total = p_c[0]
        for r in range(1, rep):
            total = total + p_c[r]
        p_sum.append(total)

    jj = lax.broadcasted_iota(jnp.int32, (n_slc, qb), 0)
    cur = t_row >> sel_shift
    forced = (jj == 0) | (jj == cur) | (jj == cur - 1)
    causal = jj <= cur
    jf = jj.astype(F32)
    imp = []
    for g in heads:
        acc = None
        for term in _split3(p_sum[g]):
            d = _dot_nt(ovl_ref[...], term)
            acc = d if acc is None else acc + d
        imp.append(jnp.where(causal, jnp.where(forced, KNOCKOUT, acc), -FORCE))
    sel = [forced.astype(F32) for _ in heads]
    for _ in range(k_sel - N_FORCED):
        for g in heads:
            mx = jnp.max(imp[g], axis=0, keepdims=True)
            first = jnp.min(jnp.where(imp[g] == mx, jf, float(n_slc)), axis=0, keepdims=True)
            pick = jf == first
            sel[g] = jnp.where(pick, 1.0, sel[g])
            imp[g] = jnp.where(pick, KNOCKOUT, imp[g])
    before_own = jj < cur
    q_aug = []
    for g in heads:
        bias_t = jnp.where((sel[g] > 0.5) & before_own, 0.0, NEG_INF)
        neg_sel = bias_t.T.astype(MXU_DTYPE)
        q_aug.append(jnp.concatenate([q4[g], jnp.concatenate([neg_sel] * rep, axis=0)], axis=1))

    w0 = pl.multiple_of(jnp.maximum(t0 - WINDOW, 0), qb)
    kp = w0 + lax.broadcasted_iota(jnp.int32, (1, win_keys), 1)
    bias_w = jnp.where((kp <= t_col) & (kp > t_col - WINDOW), 0.0, NEG_INF)
    o_w = []
    for g in heads:
        kvw = kvw_ref[g, pl.ds(w0, win_keys), :]
        e_w, inv_w = _softmax2_parts(_dot_nt(q4[g], kvw).reshape(rep, qb, win_keys) + bias_w[None])
        o_w.append(_dot(e_w.reshape(rep * qb, win_keys), kvw) * inv_w.reshape(rep * qb, 1))

    m0, acc0 = [], []
    for g in heads:
        kv_own = kvs_ref[g, pl.ds(pl.multiple_of(t0, qb), qb), :]
        ov_own_t = jnp.where(lane < dh, 1.0, kv_own.astype(F32)).T
        s_d = _dot_nt(kv_own, q4[g]) + diag_ref[...]
        m0.append(jnp.max(s_d, axis=0, keepdims=True))
        acc0.append(_dot(ov_own_t, jnp.exp2(s_d - m0[g])))
    return [(q_aug[g], o_c[g], o_w[g], m0[g], acc0[g]) for g in heads]


def _nsa(q, gl, kv_cmp, kv6):
    b, s, _ = q.shape
    qt = min(NSA_Q_TILE, s)
    nb = s // qt
    gw = NSA_REP * NSA_HEAD_DIM
    n_cmp_rows = s // CMP_STRIDE
    n_slc = s // SEL_BLOCK
    kc = min(SEL_KEY_CHUNK, s)
    assert (s // kc) % 2 == 0, "the chunk loop walks key chunks in pairs"
    key_block = jnp.arange(s, dtype=jnp.int32)[:, None] // SEL_BLOCK
    blk_onehot = (key_block == jnp.arange(n_slc, dtype=jnp.int32)[None, :]).astype(MXU_DTYPE)
    sj = jnp.arange(n_slc, dtype=jnp.int32)[:, None] * SEL_BLOCK
    ci = jnp.arange(n_cmp_rows, dtype=jnp.int32)[None, :] * CMP_STRIDE
    overlap_t = ((ci < sj + SEL_BLOCK) & (ci + CMP_BLOCK > sj)).astype(MXU_DTYPE)
    ti = jnp.arange(qt, dtype=jnp.int32)
    own_block = (ti[:, None] // SEL_BLOCK == ti[None, :] // SEL_BLOCK) & (ti[:, None] <= ti[None, :])
    diag_bias = jnp.tile(jnp.where(own_block, 0.0, NEG_INF).astype(F32), (1, NSA_REP))
    ng = NSA_KV_HEADS
    const = lambda shape: pl.BlockSpec(shape, lambda bb, i: (0,) * len(shape), pipeline_mode=pl.Buffered(1))
    return pl.pallas_call(
        functools.partial(_nsa_kernel, seq=s),
        grid=(b, nb),
        in_specs=[
            pl.BlockSpec((None, qt, ng * gw), lambda bb, i: (bb, i, 0)),
            pl.BlockSpec((None, qt, GATE_PAD), lambda bb, i: (bb, i, 0)),
            pl.BlockSpec((None, ng, n_cmp_rows, KV_PACK), lambda bb, i: (bb, 0, 0, 0)),
            pl.BlockSpec((None, ng, s, KV_PACK), lambda bb, i: (bb, 1, 0, 0)),
            pl.BlockSpec((None, ng, s, KV_PACK), lambda bb, i: (bb, 2, 0, 0)),
            const((s, n_slc)), const((n_slc, n_cmp_rows)), const((qt, NSA_REP * qt)),
        ],
        out_specs=pl.BlockSpec((None, qt, ng * gw), lambda bb, i: (bb, i, 0)),
        out_shape=jax.ShapeDtypeStruct((b, s, NSA_WIDTH), MXU_DTYPE),
        scratch_shapes=[pltpu.VMEM((ng, 2, kc, NSA_REP * qt), F32),
                        pltpu.VMEM((ng, 2, kc, NSA_REP * qt), MXU_DTYPE),
                        pltpu.VMEM((ng, s // kc, KV_PACK, kc), MXU_DTYPE)],
        compiler_params=_params("parallel", "arbitrary"),
        name="nsa_attention",
    )(q, gl, kv_cmp, kv6, kv6, blk_onehot, overlap_t, diag_bias)


def _mix_out_kernel(x_ref, nsa_ref, zg_ref, zc_ref, zch_ref, gln_g_ref, gln_b_ref, ws_ref, bs_ref,
                    dww_ref, dwb_ref, cln_g_ref, cln_b_ref, og_ref, wo_ref, o_ref, hbuf_ref):
    tm = x_ref.shape[0]
    i = pl.program_id(1)

    def glu(z):
        z = z.astype(F32)
        return z[:, :CONV_WIDTH] * jax.nn.sigmoid(z[:, CONV_WIDTH:])

    halo = glu(zch_ref[...])
    hbuf_ref[0, 0:CONV_HALO, :] = jnp.where(i == 0, 0.0, halo)
    hbuf_ref[0, CONV_HALO:CONV_HALO + tm, :] = glu(zc_ref[...])
    n_shifted = CONV_HALO + tm - F32_SUBLANES
    for c in range(1, F32_SUBLANES):
        hbuf_ref[c, 0:n_shifted, :] = hbuf_ref[0, c:c + n_shifted, :]
    conv = jnp.broadcast_to(dwb_ref[...], (tm, CONV_WIDTH))
    for k in range(CONV_TAPS):
        off = CONV_HALO - (CONV_TAPS - 1) + k
        c = off % F32_SUBLANES
        conv = conv + dww_ref[k:k + 1, :] * hbuf_ref[c, off - c:off - c + tm, :]
    y_conv = jax.nn.silu(_layer_norm(conv, cln_g_ref[...], cln_b_ref[...]))

    zg = _gelu_tanh(zg_ref[...].astype(F32))
    u = zg[:, :GMLP_WIDTH]
    v = _layer_norm(zg[:, GMLP_WIDTH:], gln_g_ref[...], gln_b_ref[...]).astype(MXU_DTYPE)
    row = lax.broadcasted_iota(jnp.int32, (GMLP_CHUNK, GMLP_CHUNK), 0)
    col = lax.broadcasted_iota(jnp.int32, (GMLP_CHUNK, GMLP_CHUNK), 1)
    grp = lax.broadcasted_iota(jnp.int32, (GMLP_CHUNK, GMLP_WIDTH), 1) // (GMLP_WIDTH // GMLP_GROUPS)
    w_causal = [jnp.where(row >= col, ws_ref[g], 0.0).astype(MXU_DTYPE) for g in range(GMLP_GROUPS)]
    bias = jnp.zeros((GMLP_CHUNK, GMLP_WIDTH), F32)
    for g in range(GMLP_GROUPS):
        bias = jnp.where(grp == g, bs_ref[:, g:g + 1], bias)
    mixed = []
    for c in range(tm // GMLP_CHUNK):
        vc = v[c * GMLP_CHUNK:(c + 1) * GMLP_CHUNK]
        acc = bias
        for g in range(GMLP_GROUPS):
            acc = acc + jnp.where(grp == g, jnp.dot(w_causal[g], vc, preferred_element_type=F32), 0.0)
        mixed.append(acc)
    y_gmlp = u * jnp.concatenate(mixed, axis=0)

    og = og_ref[...]
    n0, n1 = NSA_WIDTH, NSA_WIDTH + GMLP_WIDTH
    y = _dot(_rms(nsa_ref[...].astype(F32), og[:, :n0]), wo_ref[:n0, :])
    y = y + _dot(_rms(y_gmlp, og[:, n0:n1]), wo_ref[n0:n1, :])
    y = y + _dot(_rms(y_conv, og[:, n1:]), wo_ref[n1:, :])
    o_ref[...] = x_ref[...] + y


def _mix_out(x, y_nsa, zg, zc, p, w_out, layer):
    b, s, d = x.shape
    tm = min(MIX_ROWS, s)
    halo_blocks = tm // CONV_HALO
    row2 = lambda a: a.reshape(1, -1)
    full = lambda shape: pl.BlockSpec(shape, lambda bb, i: (0,) * len(shape))
    tile = lambda w: pl.BlockSpec((None, tm, w), lambda bb, i: (bb, i, 0))
    return pl.pallas_call(
        _mix_out_kernel,
        grid=(b, s // tm),
        in_specs=[
            tile(d), tile(NSA_WIDTH), tile(2 * GMLP_WIDTH), tile(2 * CONV_WIDTH),
            pl.BlockSpec((None, CONV_HALO, 2 * CONV_WIDTH),
                         lambda bb, i: (bb, jnp.maximum(i * halo_blocks - 1, 0), 0)),
            full((1, GMLP_WIDTH)), full((1, GMLP_WIDTH)),
            full((GMLP_GROUPS, GMLP_CHUNK, GMLP_CHUNK)), full((GMLP_CHUNK, GMLP_GROUPS)),
            full((CONV_TAPS, CONV_WIDTH)), full((1, CONV_WIDTH)), full((1, CONV_WIDTH)), full((1, CONV_WIDTH)),
            full((1, d)), _layer_spec(w_out, layer),
        ],
        out_specs=tile(d),
        out_shape=jax.ShapeDtypeStruct((b, s, d), F32),
        scratch_shapes=[pltpu.VMEM((F32_SUBLANES, CONV_HALO + tm, CONV_WIDTH), F32)],
        compiler_params=_params("parallel", "parallel"),
        name="mix_out",
    )(x, y_nsa, zg, zc, zc, row2(p["gmlp_ln_g"]), row2(p["gmlp_ln_b"]), p["gmlp_ws"], p["gmlp_bs"].T,
      p["conv_dw_w"], row2(p["conv_dw_b"]), row2(p["conv_ln_g"]), row2(p["conv_ln_b"]),
      row2(p["mix_out_g"]), w_out)


def _mem_kv_kernel(m_ref, g_ref, wk_ref, wv_ref, k_ref, v_ref):
    m = _rms(m_ref[...], g_ref[...]).astype(MXU_DTYPE)
    k_ref[...] = _dot(m, wk_ref[...]).astype(k_ref.dtype)
    v_ref[...] = _dot(m, wv_ref[...]).astype(v_ref.dtype)


def _mem_kv(mem, g, wk, wv, layer):
    b, n, d = mem.shape
    blk = pl.BlockSpec((None, n, d), lambda bb: (bb, 0, 0))
    return pl.pallas_call(
        _mem_kv_kernel,
        grid=(b,),
        in_specs=[blk, pl.BlockSpec((1, d), lambda bb: (0, 0)), _layer_spec(wk, layer), _layer_spec(wv, layer)],
        out_specs=[blk, blk],
        out_shape=[jax.ShapeDtypeStruct((b, n, d), MXU_DTYPE)] * 2,
        compiler_params=_params("parallel"),
        name="xattn_mem_kv",
    )(mem, g.reshape(1, d), wk, wv)


def _xattn_kernel(x_ref, g_ref, wq_ref, k_ref, v_ref, wo_ref, o_ref, att_ref):
    dh = XATTN_HEAD_DIM
    heads = [slice(hd * dh, (hd + 1) * dh) for hd in range(XATTN_HEADS)]
    x = x_ref[...]
    h = _rms(x, g_ref[...]).astype(MXU_DTYPE)
    q = (_dot(h, wq_ref[...]) * (dh ** -0.5 * LOG2E)).astype(MXU_DTYPE)
    s = [_dot_nt(q[:, cols], k_ref[:, cols]) for cols in heads]
    e = [jnp.exp2(sh - jnp.max(sh, axis=-1, keepdims=True)) for sh in s]
    inv = [1.0 / jnp.sum(eh, axis=-1, keepdims=True) for eh in e]
    for cols, eh, ih in zip(heads, e, inv):
        att_ref[:, cols] = (_dot(eh, v_ref[:, cols]) * ih).astype(att_ref.dtype)
    o_ref[...] = x + _dot(att_ref[...], wo_ref[...])


def _xattn(x, g, wq, k, v, wo, layer):
    b, s, d = x.shape
    n = k.shape[1]
    tm = min(XATTN_ROWS, s)
    tile = pl.BlockSpec((None, tm, d), lambda bb, i: (bb, i, 0))
    mspec = pl.BlockSpec((None, n, d), lambda bb, i: (bb, 0, 0))
    return pl.pallas_call(
        _xattn_kernel,
        grid=(b, s // tm),
        in_specs=[tile, pl.BlockSpec((1, d), lambda bb, i: (0, 0)), _layer_spec(wq, layer), mspec, mspec,
                  _layer_spec(wo, layer)],
        out_specs=tile,
        out_shape=jax.ShapeDtypeStruct((b, s, d), F32),
        scratch_shapes=[pltpu.VMEM((tm, d), MXU_DTYPE)],
        compiler_params=_params("parallel", "parallel"),
        name="xattn",
    )(x, g.reshape(1, d), wq, k, v, wo)


def _ffn_kernel(x_ref, xh_ref, g_ref, wup_ref, cw_ref, cb_ref, wd_ref, fg_ref, o_ref, xn_ref, h_buf, act_buf,
                *, final_norm):
    tm = x_ref.shape[0]
    tc = FFN_COLS
    i = pl.program_id(1)
    halo = jnp.where(i == 0, 0.0, xh_ref[...])
    xn_ref[0:FFN_HALO, :] = _rms(halo, g_ref[...]).astype(xn_ref.dtype)
    xn_ref[FFN_HALO:FFN_HALO + tm, :] = _rms(x_ref[...], g_ref[...]).astype(xn_ref.dtype)
    xn = xn_ref[...]

    def conv(slot, half, c0):
        cols = slice(c0, c0 + tc)
        y = cb_ref[:, cols]
        for k in range(FFN_CONV_TAPS):
            shift = FFN_CONV_TAPS - 1 - k
            y = y + cw_ref[k:k + 1, cols] * h_buf[slot, half, FFN_HALO - shift:FFN_HALO - shift + tm, :]
        return y

    def project(j):
        gate_c0, up_c0 = j * tc, D_FF + j * tc
        h_buf[j % 2, 0] = _dot(xn, wup_ref[:, gate_c0:gate_c0 + tc])
        h_buf[j % 2, 1] = _dot(xn, wup_ref[:, up_c0:up_c0 + tc])

    n_blocks = D_FF // tc
    project(0)
    for j in range(n_blocks):
        if j + 1 < n_blocks:
            project(j + 1)
        act = jax.nn.silu(conv(j % 2, 0, j * tc)) * conv(j % 2, 1, D_FF + j * tc)
        act_buf[:, j * tc:(j + 1) * tc] = act.astype(act_buf.dtype)

    y = x_ref[...] + _dot(act_buf[...], wd_ref[...])
    o_ref[...] = _rms(y, fg_ref[...]) if final_norm else y


def _ffn(x, g, w_up, dw_w, dw_b, w_down, final_g, final_norm, layer):
    b, s, d = x.shape
    tm = min(FFN_ROWS, s)
    halo_blocks = tm // FFN_HALO
    tile = pl.BlockSpec((None, tm, d), lambda bb, i: (bb, i, 0))
    once = dict(pipeline_mode=pl.Buffered(1))
    resident = lambda shape: pl.BlockSpec(shape, lambda bb, i: (0,) * len(shape), **once)
    return pl.pallas_call(
        functools.partial(_ffn_kernel, final_norm=final_norm),
        grid=(b, s // tm),
        in_specs=[
            tile,
            pl.BlockSpec((None, FFN_HALO, d), lambda bb, i: (bb, jnp.maximum(i * halo_blocks - 1, 0), 0)),
            resident((1, d)),
            _layer_spec(w_up, layer, **once),
            resident((FFN_CONV_TAPS, 2 * D_FF)),
            resident((1, 2 * D_FF)),
            _layer_spec(w_down, layer, **once),
            resident((1, d)),
        ],
        out_specs=tile,
        out_shape=jax.ShapeDtypeStruct((b, s, d), F32),
        scratch_shapes=[pltpu.VMEM((FFN_HALO + tm, d), MXU_DTYPE),
                        pltpu.VMEM((2, 2, FFN_HALO + tm, FFN_COLS), F32),
                        pltpu.VMEM((tm, D_FF), MXU_DTYPE)],
        compiler_params=_params("parallel", "parallel"),
        name="conv_ffn",
    )(x, x, g.reshape(1, d), w_up, dw_w, dw_b.reshape(1, -1), w_down, final_g.reshape(1, d))


def kernel(x, mem, mix_norm_g, w_in, cmp_pe, cmp_w1, cmp_w2, gmlp_ln_g, gmlp_ln_b, gmlp_ws, gmlp_bs, conv_dw_w, conv_dw_b, conv_ln_g, conv_ln_b, mix_out_g, w_out, xattn_norm_g, mem_norm_g, xattn_wq, xattn_wk, xattn_wv, xattn_wo, ffn_norm_g, ffn_w_up, ffn_dw_w, ffn_dw_b, ffn_w_down, final_norm_g):
    depth = w_in.shape[0]
    cast = lambda w: w.astype(MXU_DTYPE)
    w_in_p = _permute_w_in(w_in)
    w_out_c, wq_c, wk_c, wv_c, wo_c = cast(w_out), cast(xattn_wq), cast(xattn_wk), cast(xattn_wv), cast(xattn_wo)
    w_up_c, w_down_c = cast(ffn_w_up), cast(ffn_w_down)
    for l in range(depth):
        q, kv6, zg, zc, gl = _in_proj(x, mix_norm_g[l], w_in_p, l)
        kv_cmp = _compress(kv6, *_pack_cmp_weights(cmp_pe[l], cmp_w1[l], cmp_w2[l]))
        y_nsa = _nsa(q, gl, kv_cmp, kv6)
        mix = dict(gmlp_ln_g=gmlp_ln_g[l], gmlp_ln_b=gmlp_ln_b[l], gmlp_ws=gmlp_ws[l], gmlp_bs=gmlp_bs[l],
                   conv_dw_w=conv_dw_w[l], conv_dw_b=conv_dw_b[l], conv_ln_g=conv_ln_g[l],
                   conv_ln_b=conv_ln_b[l], mix_out_g=mix_out_g[l])
        x = _mix_out(x, y_nsa, zg, zc, mix, w_out_c, l)
        mk, mv = _mem_kv(mem, mem_norm_g[l], wk_c, wv_c, l)
        x = _xattn(x, xattn_norm_g[l], wq_c, mk, mv, wo_c, l)
        x = _ffn(x, ffn_norm_g[l], w_up_c, ffn_dw_w[l], ffn_dw_b[l], w_down_c, final_norm_g,
                 final_norm=(l == depth - 1), layer=l)
    return x
```

```python
import functools

import jax
import jax.numpy as jnp
from jax import lax
from jax.experimental import pallas as pl
from jax.experimental.pallas import tpu as pltpu

D_MODEL = 1024
NSA_HEADS = 8
NSA_KV_HEADS = 2
NSA_HEAD_DIM = 64
NSA_REP = NSA_HEADS // NSA_KV_HEADS
CMP_BLOCK = 32
CMP_STRIDE = 16
SEL_BLOCK = 64
N_SELECT = 16
WINDOW = 512
Q_BLOCK = 128
NSA_WIDTH = NSA_HEADS * NSA_HEAD_DIM
KV_WIDTH = NSA_KV_HEADS * NSA_HEAD_DIM
GMLP_WIDTH = 256
GMLP_GROUPS = 4
GMLP_CHUNK = 128
CONV_WIDTH = 256
CONV_TAPS = 31
XATTN_HEADS = 4
XATTN_HEAD_DIM = D_MODEL // XATTN_HEADS
D_FF = 2816
FFN_CONV_TAPS = 3
EPS = 1e-6
NEG_INF = -1e30
TINY = 1e-30
FORCE = 1e4
LOG2E = 1.4426950408889634
MAX_FLOOR = -1e29
KNOCKOUT = -3.0e38
N_FORCED = 3

LANES = 128
F32_SUBLANES = 8
BF16_SUBLANES = 16
VMEM_LIMIT_BYTES = 48 * 1024 * 1024

MXU_DTYPE = jnp.bfloat16
F32 = jnp.float32

IN_PROJ_ROWS = 512
NSA_Q_TILE = 256
SEL_KEY_CHUNK = 512
MIX_ROWS = 512
CONV_HALO = 32
XATTN_ROWS = 512
FFN_ROWS = 512
FFN_COLS = 256
FFN_HALO = 16
GATE_PAD = LANES
KV_PACK = 2 * NSA_HEAD_DIM


def _params(*semantics):
    return pltpu.CompilerParams(dimension_semantics=semantics, vmem_limit_bytes=VMEM_LIMIT_BYTES)


def _dot(a, b):
    return jnp.dot(a.astype(MXU_DTYPE), b.astype(MXU_DTYPE), preferred_element_type=F32)


def _dot_nt(a, b):
    return lax.dot_general(a.astype(MXU_DTYPE), b.astype(MXU_DTYPE), (((1,), (1,)), ((), ())),
                           preferred_element_type=F32)


def _rms(x, g):
    return x * lax.rsqrt(jnp.mean(x * x, axis=-1, keepdims=True) + EPS) * g


def _layer_norm(x, g, b):
    mu = jnp.mean(x, axis=-1, keepdims=True)
    xc = x - mu
    var = jnp.mean(xc * xc, axis=-1, keepdims=True)
    return xc * lax.rsqrt(var + EPS) * g + b


def _gelu_tanh(x):
    return 0.5 * x * (1.0 + jnp.tanh(0.7978845608028654 * (x + 0.044715 * (x * x * x))))


def _masked_softmax(s, valid):
    s = jnp.where(valid, s, NEG_INF)
    m = jnp.max(s, axis=-1, keepdims=True)
    e = jnp.where(valid, jnp.exp(s - m), 0.0)
    return e / jnp.maximum(jnp.sum(e, axis=-1, keepdims=True), TINY)


IN_Q = (0, NSA_WIDTH)
IN_KV = (IN_Q[1], IN_Q[1] + 6 * KV_PACK)
IN_ZG = (IN_KV[1], IN_KV[1] + 2 * GMLP_WIDTH)
IN_ZC = (IN_ZG[1], IN_ZG[1] + 2 * CONV_WIDTH)
IN_GL = (IN_ZC[1], IN_ZC[1] + GATE_PAD)
IN_COLS = IN_GL[1]


def _layer_spec(stacked, layer, **kwargs):
    shape = stacked.shape[1:]
    return pl.BlockSpec((None,) + shape, lambda *_: (layer,) + (0,) * len(shape), **kwargs)


def _permute_w_in(w):
    off_kv = NSA_WIDTH
    off_gl = NSA_WIDTH + 6 * KV_WIDTH
    off_zg = off_gl + NSA_HEADS * 3
    off_zc = off_zg + 2 * GMLP_WIDTH
    parts = [w[..., :NSA_WIDTH]]
    for branch in range(3):
        k0 = off_kv + (2 * branch) * KV_WIDTH
        v0 = off_kv + (2 * branch + 1) * KV_WIDTH
        for g in range(NSA_KV_HEADS):
            parts.append(w[..., k0 + g * NSA_HEAD_DIM:k0 + (g + 1) * NSA_HEAD_DIM])
            parts.append(w[..., v0 + g * NSA_HEAD_DIM:v0 + (g + 1) * NSA_HEAD_DIM])
    parts.append(w[..., off_zg:off_zg + 2 * GMLP_WIDTH])
    parts.append(w[..., off_zc:off_zc + 2 * CONV_WIDTH])
    n_gate = NSA_HEADS * 3
    parts.append(jnp.pad(w[..., off_gl:off_gl + n_gate], ((0, 0), (0, 0), (0, GATE_PAD - n_gate))))
    return jnp.concatenate(parts, axis=-1).astype(MXU_DTYPE)


def _in_proj_kernel(x_ref, g_ref, w_ref, q_ref, kv_ref, zg_ref, zc_ref, gl_ref):
    h = _rms(x_ref[...], g_ref[...]).astype(MXU_DTYPE)
    q_ref[...] = _dot(h, w_ref[:, IN_Q[0]:IN_Q[1]]).astype(q_ref.dtype)
    for j in range(6):
        c0 = IN_KV[0] + j * KV_PACK
        kv_ref[j] = _dot(h, w_ref[:, c0:c0 + KV_PACK]).astype(kv_ref.dtype)
    zg_ref[...] = _dot(h, w_ref[:, IN_ZG[0]:IN_ZG[1]]).astype(zg_ref.dtype)
    zc_ref[...] = _dot(h, w_ref[:, IN_ZC[0]:IN_ZC[1]]).astype(zc_ref.dtype)
    gl_ref[...] = _dot(h, w_ref[:, IN_GL[0]:IN_GL[1]])


def _in_proj(x, g, w, layer):
    b, s, d = x.shape
    tm = IN_PROJ_ROWS
    return pl.pallas_call(
        _in_proj_kernel,
        grid=(b, s // tm),
        in_specs=[
            pl.BlockSpec((None, tm, d), lambda bb, i: (bb, i, 0)),
            pl.BlockSpec((1, d), lambda bb, i: (0, 0)),
            _layer_spec(w, layer),
        ],
        out_specs=[
            pl.BlockSpec((None, tm, NSA_WIDTH), lambda bb, i: (bb, i, 0)),
            pl.BlockSpec((None, 6, tm, KV_PACK), lambda bb, i: (bb, 0, i, 0)),
            pl.BlockSpec((None, tm, 2 * GMLP_WIDTH), lambda bb, i: (bb, i, 0)),
            pl.BlockSpec((None, tm, 2 * CONV_WIDTH), lambda bb, i: (bb, i, 0)),
            pl.BlockSpec((None, tm, GATE_PAD), lambda bb, i: (bb, i, 0)),
        ],
        out_shape=[
            jax.ShapeDtypeStruct((b, s, NSA_WIDTH), MXU_DTYPE),
            jax.ShapeDtypeStruct((b, 6, s, KV_PACK), MXU_DTYPE),
            jax.ShapeDtypeStruct((b, s, 2 * GMLP_WIDTH), MXU_DTYPE),
            jax.ShapeDtypeStruct((b, s, 2 * CONV_WIDTH), MXU_DTYPE),
            jax.ShapeDtypeStruct((b, s, GATE_PAD), F32),
        ],
        compiler_params=_params("parallel", "parallel"),
        name="in_proj",
    )(x, g.reshape(1, d), w)


CMP_ROW = CMP_STRIDE * KV_PACK


def _pack_cmp_weights(pe, w1, w2):
    dh = NSA_HEAD_DIM
    w1r = w1.reshape(2, CMP_BLOCK, dh, dh)
    z = jnp.zeros_like(w1r[0])
    wk = jnp.concatenate([w1r[0], z], axis=-1)
    wv = jnp.concatenate([z, w1r[1]], axis=-1)
    w1p = jnp.concatenate([wk, wv], axis=1).reshape(CMP_BLOCK * KV_PACK, KV_PACK)
    z2 = jnp.zeros_like(w2[0])
    w2p = jnp.concatenate([jnp.concatenate([w2[0], z2], axis=1), jnp.concatenate([z2, w2[1]], axis=1)], axis=0)
    pep = jnp.concatenate([pe[0], pe[1]], axis=-1).reshape(1, CMP_BLOCK * KV_PACK)
    pep = jnp.broadcast_to(pep, (8, CMP_BLOCK * KV_PACK))
    return pep.astype(MXU_DTYPE), w1p.astype(MXU_DTYPE), w2p.astype(MXU_DTYPE)


def _compress_kernel(c_ref, pe_ref, w1_ref, w2_ref, o_ref):
    c = c_ref[...]
    n = c.shape[0]
    first = _dot(c, w1_ref[:CMP_ROW, :])
    second = _dot(c, w1_ref[CMP_ROW:, :])
    pe_term = _dot(pe_ref[...], w1_ref[...])[0:1, :]
    h = first + pltpu.roll(second, n - 1, 0) + pe_term
    o_ref[...] = _dot(_gelu_tanh(h), w2_ref[...]).astype(o_ref.dtype)


def _compress(kv6, pep, w1p, w2p):
    b, _, s, _ = kv6.shape
    n = s // CMP_STRIDE
    c = kv6[:, :NSA_KV_HEADS].reshape(b, NSA_KV_HEADS, n, CMP_ROW)
    return pl.pallas_call(
        _compress_kernel,
        grid=(b, NSA_KV_HEADS),
        in_specs=[
            pl.BlockSpec((None, None, n, CMP_ROW), lambda bb, g: (bb, g, 0, 0)),
            pl.BlockSpec((8, 2 * CMP_ROW), lambda bb, g: (0, 0)),
            pl.BlockSpec((2 * CMP_ROW, KV_PACK), lambda bb, g: (0, 0)),
            pl.BlockSpec((KV_PACK, KV_PACK), lambda bb, g: (0, 0)),
        ],
        out_specs=pl.BlockSpec((None, None, n, KV_PACK), lambda bb, g: (bb, g, 0, 0)),
        out_shape=jax.ShapeDtypeStruct((b, NSA_KV_HEADS, n, KV_PACK), MXU_DTYPE),
        compiler_params=_params("parallel", "parallel"),
        name="nsa_compress",
    )(c, pep, w1p, w2p)


def _split3(x):
    a = x.astype(MXU_DTYPE)
    r = x - a.astype(F32)
    b = r.astype(MXU_DTYPE)
    c = (r - b.astype(F32)).astype(MXU_DTYPE)
    return a, b, c


def _softmax2_parts(s):
    m = jnp.maximum(jnp.max(s, axis=-1, keepdims=True), MAX_FLOOR)
    e = jnp.exp2(s - m)
    return e, 1.0 / jnp.maximum(jnp.sum(e, axis=-1, keepdims=True), TINY)


def _nsa_kernel(q_ref, gl_ref, cmp_ref, kvs_ref, kvw_ref, blk_ref, ovl_ref, diag_ref, o_ref, s_buf, e_buf, ovt_ref,
                *, seq):
    dh, qb, rep, groups = NSA_HEAD_DIM, q_ref.shape[0], NSA_REP, NSA_KV_HEADS
    n_cmp_rows = seq // CMP_STRIDE
    n_slc = seq // SEL_BLOCK
    k_sel = min(N_SELECT, n_slc)
    kc = min(SEL_KEY_CHUNK, seq)
    win_keys = WINDOW + qb
    sel_shift = SEL_BLOCK.bit_length() - 1
    gw = rep * dh

    bi = pl.program_id(1)
    t0 = bi * qb

    @pl.when(bi == 0)
    def _():
        chunk_lanes = lax.broadcasted_iota(jnp.int32, (kc, KV_PACK), 1)
        for g in range(groups):
            for c in range(seq // kc):
                ov = jnp.where(chunk_lanes < dh, 1.0, kvs_ref[g, c * kc:(c + 1) * kc, :].astype(F32))
                ovt_ref[g, c] = ov.T.astype(ovt_ref.dtype)

    t_col = t0 + lax.broadcasted_iota(jnp.int32, (qb, 1), 0)
    t_row = t0 + lax.broadcasted_iota(jnp.int32, (1, qb), 1)
    lane = lax.broadcasted_iota(jnp.int32, (qb, KV_PACK), 1)

    prepared = _nsa_prepare(q_ref, cmp_ref, kvs_ref, kvw_ref, ovl_ref, diag_ref, t0, t_col, t_row, lane,
                            seq=seq, qb=qb)
    q_aug = [p[0] for p in prepared]

    def scores(g, c, slot):
        k0 = pl.multiple_of(c * kc, kc)
        keys = jnp.concatenate([kvs_ref[g, pl.ds(k0, kc), :], blk_ref[pl.ds(k0, kc), :]], axis=1)
        s_buf[g, slot] = _dot_nt(keys, q_aug[g])

    def softmax_step(g, slot, m):
        s = s_buf[g, slot]
        m_new = jnp.maximum(m, jnp.max(s, axis=0, keepdims=True))
        e_buf[g, slot] = jnp.exp2(s - m_new).astype(e_buf.dtype)
        return m_new, jnp.exp2(m - m_new)

    def weighted_values(g, c, slot, alpha, acc):
        return alpha * acc + _dot(ovt_ref[g, c], e_buf[g, slot])

    last_chunk = seq // kc - 1

    def pair_body(i, carry):
        c = 2 * i
        heads = range(groups)
        m, acc, alpha_prev = ([carry[g][k] for g in heads] for k in range(3))
        alpha_even, alpha_odd = [None] * groups, [None] * groups
        for g in heads:
            scores(g, c + 1, 1)
        for g in heads:
            m[g], alpha_even[g] = softmax_step(g, 0, m[g])
        for g in heads:
            acc[g] = weighted_values(g, jnp.maximum(c - 1, 0), 1, alpha_prev[g], acc[g])
        for g in heads:
            scores(g, jnp.minimum(c + 2, last_chunk), 0)
        for g in heads:
            m[g], alpha_odd[g] = softmax_step(g, 1, m[g])
        for g in heads:
            acc[g] = weighted_values(g, c, 0, alpha_even[g], acc[g])
        return tuple((m[g], acc[g], alpha_odd[g]) for g in heads)

    n_pairs = ((t0 + qb - SEL_BLOCK + kc - 1) // kc + 1) // 2
    for g in range(groups):
        scores(g, 0, 0)
        e_buf[g, 1] = jnp.zeros(e_buf.shape[2:], e_buf.dtype)
    init = tuple((p[3], p[4], jnp.ones_like(p[3])) for p in prepared)
    final = lax.fori_loop(0, n_pairs, pair_body, init)

    lane4 = lax.broadcasted_iota(jnp.int32, (rep * qb, KV_PACK), 1)
    gates = jax.nn.sigmoid(gl_ref[...])
    for g in range(groups):
        _, o_c, o_w, _, _ = prepared[g]
        _, acc_s, alpha_last = final[g]
        acc_s = weighted_values(g, 2 * n_pairs - 1, 1, alpha_last, acc_s).T
        o_s = acc_s / jnp.where(lane4 < dh, 1.0, pltpu.roll(acc_s, dh, 1))

        outs = []
        for r in range(rep):
            rows = slice(r * qb, (r + 1) * qb)
            gc = 3 * (g * rep + r)
            outs.append(gates[:, gc:gc + 1] * o_c[rows] + gates[:, gc + 1:gc + 2] * o_s[rows]
                        + gates[:, gc + 2:gc + 3] * o_w[rows])
        for p in range(rep // 2):
            pair = jnp.where(lane < dh, pltpu.roll(outs[2 * p], dh, 1), outs[2 * p + 1])
            c0 = g * gw + p * KV_PACK
            o_ref[:, c0:c0 + KV_PACK] = pair.astype(o_ref.dtype)


def _nsa_prepare(q_ref, cmp_ref, kvs_ref, kvw_ref, ovl_ref, diag_ref, t0, t_col, t_row, lane, *, seq, qb):
    dh, rep = NSA_HEAD_DIM, NSA_REP
    heads = range(NSA_KV_HEADS)
    n_cmp_rows = seq // CMP_STRIDE
    n_slc = seq // SEL_BLOCK
    k_sel = min(N_SELECT, n_slc)
    win_keys = WINDOW + qb
    sel_shift = SEL_BLOCK.bit_length() - 1
    gw = rep * dh

    q4 = []
    for g in heads:
        parts = []
        for p in range(rep // 2):
            c0 = g * gw + p * KV_PACK
            pair = q_ref[:, c0:c0 + KV_PACK].astype(F32) * (dh ** -0.5 * LOG2E)
            parts.append(jnp.where(lane < dh, pair, 0.0))
            parts.append(jnp.where(lane < dh, pltpu.roll(pair, dh, 1), 0.0))
        q4.append(jnp.concatenate(parts, axis=0).astype(MXU_DTYPE))

    n_idx = lax.broadcasted_iota(jnp.int32, (1, n_cmp_rows), 1)
    valid_c = (n_idx * CMP_STRIDE + (CMP_BLOCK - 1) <= t_col) & (n_idx < n_cmp_rows - 1)
    bias_c = jnp.where(valid_c, 0.0, NEG_INF)
    o_c, p_sum = [], []
    for g in heads:
        kvc = cmp_ref[g]
        e_c, inv_c = _softmax2_parts(_dot_nt(q4[g], kvc).reshape(rep, qb, n_cmp_rows) + bias_c[None])
        p_c = e_c * inv_c
        o_c.append(_dot(p_c.reshape(rep * qb, n_cmp_rows), kvc))
        total = p_c[0]
        for r in range(1, rep):
            total = total + p_c[r]
        p_sum.append(total)

    jj = lax.broadcasted_iota(jnp.int32, (n_slc, qb), 0)
    cur = t_row >> sel_shift
    forced = (jj == 0) | (jj == cur) | (jj == cur - 1)
    causal = jj <= cur
    jf = jj.astype(F32)
    imp = []
    for g in heads:
        acc = None
        for term in _split3(p_sum[g]):
            d = _dot_nt(ovl_ref[...], term)
            acc = d if acc is None else acc + d
        imp.append(jnp.where(causal, jnp.where(forced, KNOCKOUT, acc), -FORCE))
    sel = [forced.astype(F32) for _ in heads]
    for _ in range(k_sel - N_FORCED):
        for g in heads:
            mx = jnp.max(imp[g], axis=0, keepdims=True)
            first = jnp.min(jnp.where(imp[g] == mx, jf, float(n_slc)), axis=0, keepdims=True)
            pick = jf == first
            sel[g] = jnp.where(pick, 1.0, sel[g])
            imp[g] = jnp.where(pick, KNOCKOUT, imp[g])
    before_own = jj < cur
    q_aug = []
    for g in heads:
        bias_t = jnp.where((sel[g] > 0.5) & before_own, 0.0, NEG_INF)
        neg_sel = bias_t.T.astype(MXU_DTYPE)
        q_aug.append(jnp.concatenate([q4[g], jnp.concatenate([neg_sel] * rep, axis=0)], axis=1))

    w0 = pl.multiple_of(jnp.maximum(t0 - WINDOW, 0), qb)
    kp = w0 + lax.broadcasted_iota(jnp.int32, (1, win_keys), 1)
    bias_w = jnp.where((kp <= t_col) & (kp > t_col - WINDOW), 0.0, NEG_INF)
    o_w = []
    for g in heads:
        kvw = kvw_ref[g, pl.ds(w0, win_keys), :]
        e_w, inv_w = _softmax2_parts(_dot_nt(q4[g], kvw).reshape(rep, qb, win_keys) + bias_w[None])
        o_w.append(_dot(e_w.reshape(rep * qb, win_keys), kvw) * inv_w.reshape(rep * qb, 1))

    m0, acc0 = [], []
    for g in heads:
        kv_own = kvs_ref[g, pl.ds(pl.multiple_of(t0, qb), qb), :]
        ov_own_t = jnp.where(lane < dh, 1.0, kv_own.astype(F32)).T
        s_d = _dot_nt(kv_own, q4[g]) + diag_ref[...]
        m0.append(jnp.max(s_d, axis=0, keepdims=True))
        acc0.append(_dot(ov_own_t, jnp.exp2(s_d - m0[g])))
    return [(q_aug[g], o_c[g], o_w[g], m0[g], acc0[g]) for g in heads]


def _nsa(q, gl, kv_cmp, kv6):
    b, s, _ = q.shape
    qt = min(NSA_Q_TILE, s)
    nb = s // qt
    gw = NSA_REP * NSA_HEAD_DIM
    n_cmp_rows = s // CMP_STRIDE
    n_slc = s // SEL_BLOCK
    kc = min(SEL_KEY_CHUNK, s)
    assert (s // kc) % 2 == 0, "the chunk loop walks key chunks in pairs"
    key_block = jnp.arange(s, dtype=jnp.int32)[:, None] // SEL_BLOCK
    blk_onehot = (key_block == jnp.arange(n_slc, dtype=jnp.int32)[None, :]).astype(MXU_DTYPE)
    sj = jnp.arange(n_slc, dtype=jnp.int32)[:, None] * SEL_BLOCK
    ci = jnp.arange(n_cmp_rows, dtype=jnp.int32)[None, :] * CMP_STRIDE
    overlap_t = ((ci < sj + SEL_BLOCK) & (ci + CMP_BLOCK > sj)).astype(MXU_DTYPE)
    ti = jnp.arange(qt, dtype=jnp.int32)
    own_block = (ti[:, None] // SEL_BLOCK == ti[None, :] // SEL_BLOCK) & (ti[:, None] <= ti[None, :])
    diag_bias = jnp.tile(jnp.where(own_block, 0.0, NEG_INF).astype(F32), (1, NSA_REP))
    ng = NSA_KV_HEADS
    const = lambda shape: pl.BlockSpec(shape, lambda bb, i: (0,) * len(shape), pipeline_mode=pl.Buffered(1))
    return pl.pallas_call(
        functools.partial(_nsa_kernel, seq=s),
        grid=(b, nb),
        in_specs=[
            pl.BlockSpec((None, qt, ng * gw), lambda bb, i: (bb, i, 0)),
            pl.BlockSpec((None, qt, GATE_PAD), lambda bb, i: (bb, i, 0)),
            pl.BlockSpec((None, ng, n_cmp_rows, KV_PACK), lambda bb, i: (bb, 0, 0, 0)),
            pl.BlockSpec((None, ng, s, KV_PACK), lambda bb, i: (bb, 1, 0, 0)),
            pl.BlockSpec((None, ng, s, KV_PACK), lambda bb, i: (bb, 2, 0, 0)),
            const((s, n_slc)), const((n_slc, n_cmp_rows)), const((qt, NSA_REP * qt)),
        ],
        out_specs=pl.BlockSpec((None, qt, ng * gw), lambda bb, i: (bb, i, 0)),
        out_shape=jax.ShapeDtypeStruct((b, s, NSA_WIDTH), MXU_DTYPE),
        scratch_shapes=[pltpu.VMEM((ng, 2, kc, NSA_REP * qt), F32),
                        pltpu.VMEM((ng, 2, kc, NSA_REP * qt), MXU_DTYPE),
                        pltpu.VMEM((ng, s // kc, KV_PACK, kc), MXU_DTYPE)],
        compiler_params=_params("parallel", "arbitrary"),
        name="nsa_attention",
    )(q, gl, kv_cmp, kv6, kv6, blk_onehot, overlap_t, diag_bias)


def _mix_out_kernel(x_ref, nsa_ref, zg_ref, zc_ref, zch_ref, gln_g_ref, gln_b_ref, ws_ref, bs_ref,
                    dww_ref, dwb_ref, cln_g_ref, cln_b_ref, og_ref, wo_ref, o_ref, hbuf_ref):
    tm = x_ref.shape[0]
    i = pl.program_id(1)

    def glu(z):
        z = z.astype(F32)
        return z[:, :CONV_WIDTH] * jax.nn.sigmoid(z[:, CONV_WIDTH:])

    halo = glu(zch_ref[...])
    hbuf_ref[0, 0:CONV_HALO, :] = jnp.where(i == 0, 0.0, halo)
    hbuf_ref[0, CONV_HALO:CONV_HALO + tm, :] = glu(zc_ref[...])
    n_shifted = CONV_HALO + tm - F32_SUBLANES
    for c in range(1, F32_SUBLANES):
        hbuf_ref[c, 0:n_shifted, :] = hbuf_ref[0, c:c + n_shifted, :]
    conv = jnp.broadcast_to(dwb_ref[...], (tm, CONV_WIDTH))
    for k in range(CONV_TAPS):
        off = CONV_HALO - (CONV_TAPS - 1) + k
        c = off % F32_SUBLANES
        conv = conv + dww_ref[k:k + 1, :] * hbuf_ref[c, off - c:off - c + tm, :]
    y_conv = jax.nn.silu(_layer_norm(conv, cln_g_ref[...], cln_b_ref[...]))

    zg = _gelu_tanh(zg_ref[...].astype(F32))
    u = zg[:, :GMLP_WIDTH]
    v = _layer_norm(zg[:, GMLP_WIDTH:], gln_g_ref[...], gln_b_ref[...]).astype(MXU_DTYPE)
    row = lax.broadcasted_iota(jnp.int32, (GMLP_CHUNK, GMLP_CHUNK), 0)
    col = lax.broadcasted_iota(jnp.int32, (GMLP_CHUNK, GMLP_CHUNK), 1)
    grp = lax.broadcasted_iota(jnp.int32, (GMLP_CHUNK, GMLP_WIDTH), 1) // (GMLP_WIDTH // GMLP_GROUPS)
    w_causal = [jnp.where(row >= col, ws_ref[g], 0.0).astype(MXU_DTYPE) for g in range(GMLP_GROUPS)]
    bias = jnp.zeros((GMLP_CHUNK, GMLP_WIDTH), F32)
    for g in range(GMLP_GROUPS):
        bias = jnp.where(grp == g, bs_ref[:, g:g + 1], bias)
    mixed = []
    for c in range(tm // GMLP_CHUNK):
        vc = v[c * GMLP_CHUNK:(c + 1) * GMLP_CHUNK]
        acc = bias
        for g in range(GMLP_GROUPS):
            acc = acc + jnp.where(grp == g, jnp.dot(w_causal[g], vc, preferred_element_type=F32), 0.0)
        mixed.append(acc)
    y_gmlp = u * jnp.concatenate(mixed, axis=0)

    og = og_ref[...]
    n0, n1 = NSA_WIDTH, NSA_WIDTH + GMLP_WIDTH
    y = _dot(_rms(nsa_ref[...].astype(F32), og[:, :n0]), wo_ref[:n0, :])
    y = y + _dot(_rms(y_gmlp, og[:, n0:n1]), wo_ref[n0:n1, :])
    y = y + _dot(_rms(y_conv, og[:, n1:]), wo_ref[n1:, :])
    o_ref[...] = x_ref[...] + y


def _mix_out(x, y_nsa, zg, zc, p, w_out, layer):
    b, s, d = x.shape
    tm = min(MIX_ROWS, s)
    halo_blocks = tm // CONV_HALO
    row2 = lambda a: a.reshape(1, -1)
    full = lambda shape: pl.BlockSpec(shape, lambda bb, i: (0,) * len(shape))
    tile = lambda w: pl.BlockSpec((None, tm, w), lambda bb, i: (bb, i, 0))
    return pl.pallas_call(
        _mix_out_kernel,
        grid=(b, s // tm),
        in_specs=[
            tile(d), tile(NSA_WIDTH), tile(2 * GMLP_WIDTH), tile(2 * CONV_WIDTH),
            pl.BlockSpec((None, CONV_HALO, 2 * CONV_WIDTH),
                         lambda bb, i: (bb, jnp.maximum(i * halo_blocks - 1, 0), 0)),
            full((1, GMLP_WIDTH)), full((1, GMLP_WIDTH)),
            full((GMLP_GROUPS, GMLP_CHUNK, GMLP_CHUNK)), full((GMLP_CHUNK, GMLP_GROUPS)),
            full((CONV_TAPS, CONV_WIDTH)), full((1, CONV_WIDTH)), full((1, CONV_WIDTH)), full((1, CONV_WIDTH)),
            full((1, d)), _layer_spec(w_out, layer),
        ],
        out_specs=tile(d),
        out_shape=jax.ShapeDtypeStruct((b, s, d), F32),
        scratch_shapes=[pltpu.VMEM((F32_SUBLANES, CONV_HALO + tm, CONV_WIDTH), F32)],
        compiler_params=_params("parallel", "parallel"),
        name="mix_out",
    )(x, y_nsa, zg, zc, zc, row2(p["gmlp_ln_g"]), row2(p["gmlp_ln_b"]), p["gmlp_ws"], p["gmlp_bs"].T,
      p["conv_dw_w"], row2(p["conv_dw_b"]), row2(p["conv_ln_g"]), row2(p["conv_ln_b"]),
      row2(p["mix_out_g"]), w_out)


def _mem_kv_kernel(m_ref, g_ref, wk_ref, wv_ref, k_ref, v_ref):
    m = _rms(m_ref[...], g_ref[...]).astype(MXU_DTYPE)
    k_ref[...] = _dot(m, wk_ref[...]).astype(k_ref.dtype)
    v_ref[...] = _dot(m, wv_ref[...]).astype(v_ref.dtype)


def _mem_kv(mem, g, wk, wv, layer):
    b, n, d = mem.shape
    blk = pl.BlockSpec((None, n, d), lambda bb: (bb, 0, 0))
    return pl.pallas_call(
        _mem_kv_kernel,
        grid=(b,),
        in_specs=[blk, pl.BlockSpec((1, d), lambda bb: (0, 0)), _layer_spec(wk, layer), _layer_spec(wv, layer)],
        out_specs=[blk, blk],
        out_shape=[jax.ShapeDtypeStruct((b, n, d), MXU_DTYPE)] * 2,
        compiler_params=_params("parallel"),
        name="xattn_mem_kv",
    )(mem, g.reshape(1, d), wk, wv)


def _xattn_kernel(x_ref, g_ref, wq_ref, k_ref, v_ref, wo_ref, o_ref, att_ref):
    dh = XATTN_HEAD_DIM
    heads = [slice(hd * dh, (hd + 1) * dh) for hd in range(XATTN_HEADS)]
    x = x_ref[...]
    h = _rms(x, g_ref[...]).astype(MXU_DTYPE)
    q = (_dot(h, wq_ref[...]) * (dh ** -0.5 * LOG2E)).astype(MXU_DTYPE)
    s = [_dot_nt(q[:, cols], k_ref[:, cols]) for cols in heads]
    e = [jnp.exp2(sh - jnp.max(sh, axis=-1, keepdims=True)) for sh in s]
    inv = [1.0 / jnp.sum(eh, axis=-1, keepdims=True) for eh in e]
    for cols, eh, ih in zip(heads, e, inv):
        att_ref[:, cols] = (_dot(eh, v_ref[:, cols]) * ih).astype(att_ref.dtype)
    o_ref[...] = x + _dot(att_ref[...], wo_ref[...])


def _xattn(x, g, wq, k, v, wo, layer):
    b, s, d = x.shape
    n = k.shape[1]
    tm = min(XATTN_ROWS, s)
    tile = pl.BlockSpec((None, tm, d), lambda bb, i: (bb, i, 0))
    mspec = pl.BlockSpec((None, n, d), lambda bb, i: (bb, 0, 0))
    return pl.pallas_call(
        _xattn_kernel,
        grid=(b, s // tm),
        in_specs=[tile, pl.BlockSpec((1, d), lambda bb, i: (0, 0)), _layer_spec(wq, layer), mspec, mspec,
                  _layer_spec(wo, layer)],
        out_specs=tile,
        out_shape=jax.ShapeDtypeStruct((b, s, d), F32),
        scratch_shapes=[pltpu.VMEM((tm, d), MXU_DTYPE)],
        compiler_params=_params("parallel", "parallel"),
        name="xattn",
    )(x, g.reshape(1, d), wq, k, v, wo)


def _ffn_kernel(x_ref, xh_ref, g_ref, wup_ref, cw_ref, cb_ref, wd_ref, fg_ref, o_ref, xn_ref, h_buf, act_buf,
                *, final_norm):
    tm = x_ref.shape[0]
    tc = FFN_COLS
    i = pl.program_id(1)
    halo = jnp.where(i == 0, 0.0, xh_ref[...])
    xn_ref[0:FFN_HALO, :] = _rms(halo, g_ref[...]).astype(xn_ref.dtype)
    xn_ref[FFN_HALO:FFN_HALO + tm, :] = _rms(x_ref[...], g_ref[...]).astype(xn_ref.dtype)
    xn = xn_ref[...]

    def conv(slot, half, c0):
        cols = slice(c0, c0 + tc)
        y = cb_ref[:, cols]
        for k in range(FFN_CONV_TAPS):
            shift = FFN_CONV_TAPS - 1 - k
            y = y + cw_ref[k:k + 1, cols] * h_buf[slot, half, FFN_HALO - shift:FFN_HALO - shift + tm, :]
        return y

    def project(j):
        gate_c0, up_c0 = j * tc, D_FF + j * tc
        h_buf[j % 2, 0] = _dot(xn, wup_ref[:, gate_c0:gate_c0 + tc])
        h_buf[j % 2, 1] = _dot(xn, wup_ref[:, up_c0:up_c0 + tc])

    n_blocks = D_FF // tc
    project(0)
    for j in range(n_blocks):
        if j + 1 < n_blocks:
            project(j + 1)
        act = jax.nn.silu(conv(j % 2, 0, j * tc)) * conv(j % 2, 1, D_FF + j * tc)
        act_buf[:, j * tc:(j + 1) * tc] = act.astype(act_buf.dtype)

    y = x_ref[...] + _dot(act_buf[...], wd_ref[...])
    o_ref[...] = _rms(y, fg_ref[...]) if final_norm else y


def _ffn(x, g, w_up, dw_w, dw_b, w_down, final_g, final_norm, layer):
    b, s, d = x.shape
    tm = min(FFN_ROWS, s)
    halo_blocks = tm // FFN_HALO
    tile = pl.BlockSpec((None, tm, d), lambda bb, i: (bb, i, 0))
    once = dict(pipeline_mode=pl.Buffered(1))
    resident = lambda shape: pl.BlockSpec(shape, lambda bb, i: (0,) * len(shape), **once)
    return pl.pallas_call(
        functools.partial(_ffn_kernel, final_norm=final_norm),
        grid=(b, s // tm),
        in_specs=[
            tile,
            pl.BlockSpec((None, FFN_HALO, d), lambda bb, i: (bb, jnp.maximum(i * halo_blocks - 1, 0), 0)),
            resident((1, d)),
            _layer_spec(w_up, layer, **once),
            resident((FFN_CONV_TAPS, 2 * D_FF)),
            resident((1, 2 * D_FF)),
            _layer_spec(w_down, layer, **once),
            resident((1, d)),
        ],
        out_specs=tile,
        out_shape=jax.ShapeDtypeStruct((b, s, d), F32),
        scratch_shapes=[pltpu.VMEM((FFN_HALO + tm, d), MXU_DTYPE),
                        pltpu.VMEM((2, 2, FFN_HALO + tm, FFN_COLS), F32),
                        pltpu.VMEM((tm, D_FF), MXU_DTYPE)],
        compiler_params=_params("parallel", "parallel"),
        name="conv_ffn",
    )(x, x, g.reshape(1, d), w_up, dw_w, dw_b.reshape(1, -1), w_down, final_g.reshape(1, d))


def kernel(x, mem, mix_norm_g, w_in, cmp_pe, cmp_w1, cmp_w2, gmlp_ln_g, gmlp_ln_b, gmlp_ws, gmlp_bs, conv_dw_w, conv_dw_b, conv_ln_g, conv_ln_b, mix_out_g, w_out, xattn_norm_g, mem_norm_g, xattn_wq, xattn_wk, xattn_wv, xattn_wo, ffn_norm_g, ffn_w_up, ffn_dw_w, ffn_dw_b, ffn_w_down, final_norm_g):
    depth = w_in.shape[0]
    cast = lambda w: w.astype(MXU_DTYPE)
    w_in_p = _permute_w_in(w_in)
    w_out_c, wq_c, wk_c, wv_c, wo_c = cast(w_out), cast(xattn_wq), cast(xattn_wk), cast(xattn_wv), cast(xattn_wo)
    w_up_c, w_down_c = cast(ffn_w_up), cast(ffn_w_down)
    for l in range(depth):
        q, kv6, zg, zc, gl = _in_proj(x, mix_norm_g[l], w_in_p, l)
        kv_cmp = _compress(kv6, *_pack_cmp_weights(cmp_pe[l], cmp_w1[l], cmp_w2[l]))
        y_nsa = _nsa(q, gl, kv_cmp, kv6)
        mix = dict(gmlp_ln_g=gmlp_ln_g[l], gmlp_ln_b=gmlp_ln_b[l], gmlp_ws=gmlp_ws[l], gmlp_bs=gmlp_bs[l],
                   conv_dw_w=conv_dw_w[l], conv_dw_b=conv_dw_b[l], conv_ln_g=conv_ln_g[l],
                   conv_ln_b=conv_ln_b[l], mix_out_g=mix_out_g[l])
        x = _mix_out(x, y_nsa, zg, zc, mix, w_out_c, l)
        mk, mv = _mem_kv(mem, mem_norm_g[l], wk_c, wv_c, l)
        x = _xattn(x, xattn_norm_g[l], wq_c, mk, mv, wo_c, l)
        x = _ffn(x, ffn_norm_g[l], w_up_c, ffn_dw_w[l], ffn_dw_b[l], w_down_c, final_norm_g,
                 final_norm=(l == depth - 1), layer=l)
    return x
```

```python
import functools

import jax
import jax.numpy as jnp
from jax import lax
from jax.experimental import pallas as pl
from jax.experimental.pallas import tpu as pltpu

D_MODEL = 1024
NSA_HEADS = 8
NSA_KV_HEADS = 2
NSA_HEAD_DIM = 64
NSA_REP = NSA_HEADS // NSA_KV_HEADS
CMP_BLOCK = 32
CMP_STRIDE = 16
SEL_BLOCK = 64
N_SELECT = 16
WINDOW = 512
NSA_WIDTH = NSA_HEADS * NSA_HEAD_DIM
KV_WIDTH = NSA_KV_HEADS * NSA_HEAD_DIM
GMLP_WIDTH = 256
GMLP_GROUPS = 4
GMLP_CHUNK = 128
CONV_WIDTH = 256
CONV_TAPS = 31
XATTN_HEADS = 4
XATTN_HEAD_DIM = D_MODEL // XATTN_HEADS
D_FF = 2816
FFN_CONV_TAPS = 3
EPS = 1e-6
NEG_INF = -1e30
TINY = 1e-30
FORCE = 1e4
LOG2E = 1.4426950408889634
MAX_FLOOR = -1e29
KNOCKOUT = -3.0e38
N_FORCED = 3

LANES = 128
F32_SUBLANES = 8
BF16_SUBLANES = 16
VMEM_LIMIT_BYTES = 48 * 1024 * 1024

MXU_DTYPE = jnp.bfloat16
F32 = jnp.float32

IN_PROJ_ROWS = 512
NSA_Q_TILE = 256
SEL_KEY_CHUNK = 512
MIX_ROWS = 512
CONV_HALO = 32
XATTN_ROWS = 512
FFN_ROWS = 512
FFN_COLS = 256
FFN_HALO = BF16_SUBLANES
GATE_PAD = LANES
KV_PACK = 2 * NSA_HEAD_DIM


def _params(*semantics):
    return pltpu.CompilerParams(dimension_semantics=semantics, vmem_limit_bytes=VMEM_LIMIT_BYTES)


def _dot(a, b):
    return jnp.dot(a.astype(MXU_DTYPE), b.astype(MXU_DTYPE), preferred_element_type=F32)


def _dot_nt(a, b):
    return lax.dot_general(a.astype(MXU_DTYPE), b.astype(MXU_DTYPE), (((1,), (1,)), ((), ())),
                           preferred_element_type=F32)


def _rms(x, g):
    return x * lax.rsqrt(jnp.mean(x * x, axis=-1, keepdims=True) + EPS) * g


def _layer_norm(x, g, b):
    mu = jnp.mean(x, axis=-1, keepdims=True)
    xc = x - mu
    var = jnp.mean(xc * xc, axis=-1, keepdims=True)
    return xc * lax.rsqrt(var + EPS) * g + b


def _gelu_tanh(x):
    return 0.5 * x * (1.0 + jnp.tanh(0.7978845608028654 * (x + 0.044715 * (x * x * x))))


IN_Q = (0, NSA_WIDTH)
IN_KV = (IN_Q[1], IN_Q[1] + 6 * KV_PACK)
IN_ZG = (IN_KV[1], IN_KV[1] + 2 * GMLP_WIDTH)
IN_ZC = (IN_ZG[1], IN_ZG[1] + 2 * CONV_WIDTH)
IN_GL = (IN_ZC[1], IN_ZC[1] + GATE_PAD)


def _layer_spec(stacked, layer, **kwargs):
    shape = stacked.shape[1:]
    return pl.BlockSpec((None,) + shape, lambda *_: (layer,) + (0,) * len(shape), **kwargs)


def _permute_w_in(w):
    off_kv = NSA_WIDTH
    off_gl = NSA_WIDTH + 6 * KV_WIDTH
    off_zg = off_gl + NSA_HEADS * 3
    off_zc = off_zg + 2 * GMLP_WIDTH
    parts = [w[..., :NSA_WIDTH]]
    for branch in range(3):
        k0 = off_kv + (2 * branch) * KV_WIDTH
        v0 = off_kv + (2 * branch + 1) * KV_WIDTH
        for g in range(NSA_KV_HEADS):
            parts.append(w[..., k0 + g * NSA_HEAD_DIM:k0 + (g + 1) * NSA_HEAD_DIM])
            parts.append(w[..., v0 + g * NSA_HEAD_DIM:v0 + (g + 1) * NSA_HEAD_DIM])
    parts.append(w[..., off_zg:off_zg + 2 * GMLP_WIDTH])
    parts.append(w[..., off_zc:off_zc + 2 * CONV_WIDTH])
    n_gate = NSA_HEADS * 3
    parts.append(jnp.pad(w[..., off_gl:off_gl + n_gate], ((0, 0), (0, 0), (0, GATE_PAD - n_gate))))
    return jnp.concatenate(parts, axis=-1).astype(MXU_DTYPE)


def _in_proj_kernel(x_ref, g_ref, w_ref, q_ref, kv_ref, zg_ref, zc_ref, gl_ref):
    h = _rms(x_ref[...], g_ref[...]).astype(MXU_DTYPE)
    q_ref[...] = _dot(h, w_ref[:, IN_Q[0]:IN_Q[1]]).astype(q_ref.dtype)
    kv = _dot(h, w_ref[:, IN_KV[0]:IN_KV[1]])
    for j in range(6):
        kv_ref[j] = kv[:, j * KV_PACK:(j + 1) * KV_PACK].astype(kv_ref.dtype)
    zg_ref[...] = _dot(h, w_ref[:, IN_ZG[0]:IN_ZG[1]]).astype(zg_ref.dtype)
    zc_ref[...] = _dot(h, w_ref[:, IN_ZC[0]:IN_ZC[1]]).astype(zc_ref.dtype)
    gl_ref[...] = _dot(h, w_ref[:, IN_GL[0]:IN_GL[1]])


def _in_proj(x, g, w, layer):
    b, s, d = x.shape
    tm = IN_PROJ_ROWS
    return pl.pallas_call(
        _in_proj_kernel,
        grid=(b, s // tm),
        in_specs=[
            pl.BlockSpec((None, tm, d), lambda bb, i: (bb, i, 0)),
            pl.BlockSpec((1, d), lambda bb, i: (0, 0)),
            _layer_spec(w, layer),
        ],
        out_specs=[
            pl.BlockSpec((None, tm, NSA_WIDTH), lambda bb, i: (bb, i, 0)),
            pl.BlockSpec((None, 6, tm, KV_PACK), lambda bb, i: (bb, 0, i, 0)),
            pl.BlockSpec((None, tm, 2 * GMLP_WIDTH), lambda bb, i: (bb, i, 0)),
            pl.BlockSpec((None, tm, 2 * CONV_WIDTH), lambda bb, i: (bb, i, 0)),
            pl.BlockSpec((None, tm, GATE_PAD), lambda bb, i: (bb, i, 0)),
        ],
        out_shape=[
            jax.ShapeDtypeStruct((b, s, NSA_WIDTH), MXU_DTYPE),
            jax.ShapeDtypeStruct((b, 6, s, KV_PACK), MXU_DTYPE),
            jax.ShapeDtypeStruct((b, s, 2 * GMLP_WIDTH), MXU_DTYPE),
            jax.ShapeDtypeStruct((b, s, 2 * CONV_WIDTH), MXU_DTYPE),
            jax.ShapeDtypeStruct((b, s, GATE_PAD), F32),
        ],
        compiler_params=_params("parallel", "parallel"),
        name="in_proj",
    )(x, g.reshape(1, d), w)


CMP_ROW = CMP_STRIDE * KV_PACK


def _pack_cmp_weights(pe, w1, w2):
    dh = NSA_HEAD_DIM
    w1r = w1.reshape(2, CMP_BLOCK, dh, dh)
    z = jnp.zeros_like(w1r[0])
    wk = jnp.concatenate([w1r[0], z], axis=-1)
    wv = jnp.concatenate([z, w1r[1]], axis=-1)
    w1p = jnp.concatenate([wk, wv], axis=1).reshape(CMP_BLOCK * KV_PACK, KV_PACK)
    z2 = jnp.zeros_like(w2[0])
    w2p = jnp.concatenate([jnp.concatenate([w2[0], z2], axis=1), jnp.concatenate([z2, w2[1]], axis=1)], axis=0)
    pep = jnp.concatenate([pe[0], pe[1]], axis=-1).reshape(1, CMP_BLOCK * KV_PACK)
    pep = jnp.broadcast_to(pep, (F32_SUBLANES, CMP_BLOCK * KV_PACK))
    return pep.astype(MXU_DTYPE), w1p.astype(MXU_DTYPE), w2p.astype(MXU_DTYPE)


def _compress_kernel(c_ref, pe_ref, w1_ref, w2_ref, o_ref):
    c = c_ref[...]
    n = c.shape[0]
    first = _dot(c, w1_ref[:CMP_ROW, :])
    second = _dot(c, w1_ref[CMP_ROW:, :])
    pe_term = _dot(pe_ref[...], w1_ref[...])[0:1, :]
    h = first + pltpu.roll(second, n - 1, 0) + pe_term
    o_ref[...] = _dot(_gelu_tanh(h), w2_ref[...]).astype(o_ref.dtype)


def _compress(kv6, pep, w1p, w2p):
    b, _, s, _ = kv6.shape
    n = s // CMP_STRIDE
    c = kv6[:, :NSA_KV_HEADS].reshape(b, NSA_KV_HEADS, n, CMP_ROW)
    return pl.pallas_call(
        _compress_kernel,
        grid=(b, NSA_KV_HEADS),
        in_specs=[
            pl.BlockSpec((None, None, n, CMP_ROW), lambda bb, g: (bb, g, 0, 0)),
            pl.BlockSpec((F32_SUBLANES, 2 * CMP_ROW), lambda bb, g: (0, 0)),
            pl.BlockSpec((2 * CMP_ROW, KV_PACK), lambda bb, g: (0, 0)),
            pl.BlockSpec((KV_PACK, KV_PACK), lambda bb, g: (0, 0)),
        ],
        out_specs=pl.BlockSpec((None, None, n, KV_PACK), lambda bb, g: (bb, g, 0, 0)),
        out_shape=jax.ShapeDtypeStruct((b, NSA_KV_HEADS, n, KV_PACK), MXU_DTYPE),
        compiler_params=_params("parallel", "parallel"),
        name="nsa_compress",
    )(c, pep, w1p, w2p)


def _split3(x):
    a = x.astype(MXU_DTYPE)
    r = x - a.astype(F32)
    b = r.astype(MXU_DTYPE)
    c = (r - b.astype(F32)).astype(MXU_DTYPE)
    return a, b, c


def _softmax2_parts(s):
    m = jnp.maximum(jnp.max(s, axis=-1, keepdims=True), MAX_FLOOR)
    e = jnp.exp2(s - m)
    return e, 1.0 / jnp.maximum(jnp.sum(e, axis=-1, keepdims=True), TINY)


def _nsa_kernel(q_ref, gl_ref, cmp_ref, kvs_ref, kvw_ref, blk_ref, ovl_ref, diag_ref, o_ref, s_buf, e_buf, ovt_ref,
                *, seq):
    dh, qb, rep, groups = NSA_HEAD_DIM, q_ref.shape[0], NSA_REP, NSA_KV_HEADS
    n_cmp_rows = seq // CMP_STRIDE
    n_slc = seq // SEL_BLOCK
    k_sel = min(N_SELECT, n_slc)
    kc = min(SEL_KEY_CHUNK, seq)
    win_keys = WINDOW + qb
    sel_shift = SEL_BLOCK.bit_length() - 1
    gw = rep * dh

    bi = pl.program_id(1)
    t0 = bi * qb

    @pl.when(bi == 0)
    def _():
        chunk_lanes = lax.broadcasted_iota(jnp.int32, (kc, KV_PACK), 1)
        for g in range(groups):
            for c in range(seq // kc):
                ov = jnp.where(chunk_lanes < dh, 1.0, kvs_ref[g, c * kc:(c + 1) * kc, :].astype(F32))
                ovt_ref[g, c] = ov.T.astype(ovt_ref.dtype)

    t_col = t0 + lax.broadcasted_iota(jnp.int32, (qb, 1), 0)
    t_row = t0 + lax.broadcasted_iota(jnp.int32, (1, qb), 1)
    lane = lax.broadcasted_iota(jnp.int32, (qb, KV_PACK), 1)

    prepared = _nsa_prepare(q_ref, cmp_ref, kvs_ref, kvw_ref, ovl_ref, diag_ref, t0, t_col, t_row, lane,
                            seq=seq, qb=qb)
    q_aug = [p[0] for p in prepared]

    def scores(g, c, slot):
        k0 = pl.multiple_of(c * kc, kc)
        keys = jnp.concatenate([kvs_ref[g, pl.ds(k0, kc), :], blk_ref[pl.ds(k0, kc), :]], axis=1)
        s_buf[g, slot] = _dot_nt(keys, q_aug[g])

    def softmax_step(g, slot, m):
        s = s_buf[g, slot]
        m_new = jnp.maximum(m, jnp.max(s, axis=0, keepdims=True))
        e_buf[g, slot] = jnp.exp2(s - m_new).astype(e_buf.dtype)
        return m_new, jnp.exp2(m - m_new)

    def weighted_values(g, c, slot, alpha, acc):
        return alpha * acc + _dot(ovt_ref[g, c], e_buf[g, slot])

    last_chunk = seq // kc - 1

    def pair_body(i, carry):
        c = 2 * i
        heads = range(groups)
        m, acc, alpha_prev = ([carry[g][k] for g in heads] for k in range(3))
        alpha_even, alpha_odd = [None] * groups, [None] * groups
        for g in heads:
            scores(g, c + 1, 1)
        for g in heads:
            m[g], alpha_even[g] = softmax_step(g, 0, m[g])
        for g in heads:
            acc[g] = weighted_values(g, jnp.maximum(c - 1, 0), 1, alpha_prev[g], acc[g])
        for g in heads:
            scores(g, jnp.minimum(c + 2, last_chunk), 0)
        for g in heads:
            m[g], alpha_odd[g] = softmax_step(g, 1, m[g])
        for g in heads:
            acc[g] = weighted_values(g, c, 0, alpha_even[g], acc[g])
        return tuple((m[g], acc[g], alpha_odd[g]) for g in heads)

    n_pairs = ((t0 + qb - SEL_BLOCK + kc - 1) // kc + 1) // 2
    for g in range(groups):
        scores(g, 0, 0)
        e_buf[g, 1] = jnp.zeros(e_buf.shape[2:], e_buf.dtype)
    init = tuple((p[3], p[4], jnp.ones_like(p[3])) for p in prepared)
    final = lax.fori_loop(0, n_pairs, pair_body, init)

    lane4 = lax.broadcasted_iota(jnp.int32, (rep * qb, KV_PACK), 1)
    gates = jax.nn.sigmoid(gl_ref[...])
    for g in range(groups):
        _, o_c, o_w, _, _ = prepared[g]
        _, acc_s, alpha_last = final[g]
        acc_s = weighted_values(g, 2 * n_pairs - 1, 1, alpha_last, acc_s).T
        o_s = acc_s / jnp.where(lane4 < dh, 1.0, pltpu.roll(acc_s, dh, 1))

        outs = []
        for r in range(rep):
            rows = slice(r * qb, (r + 1) * qb)
            gc = 3 * (g * rep + r)
            outs.append(gates[:, gc:gc + 1] * o_c[rows] + gates[:, gc + 1:gc + 2] * o_s[rows]
                        + gates[:, gc + 2:gc + 3] * o_w[rows])
        for p in range(rep // 2):
            pair = jnp.where(lane < dh, pltpu.roll(outs[2 * p], dh, 1), outs[2 * p + 1])
            c0 = g * gw + p * KV_PACK
            o_ref[:, c0:c0 + KV_PACK] = pair.astype(o_ref.dtype)


def _nsa_prepare(q_ref, cmp_ref, kvs_ref, kvw_ref, ovl_ref, diag_ref, t0, t_col, t_row, lane, *, seq, qb):
    dh, rep = NSA_HEAD_DIM, NSA_REP
    heads = range(NSA_KV_HEADS)
    n_cmp_rows = seq // CMP_STRIDE
    n_slc = seq // SEL_BLOCK
    k_sel = min(N_SELECT, n_slc)
    win_keys = WINDOW + qb
    sel_shift = SEL_BLOCK.bit_length() - 1
    gw = rep * dh

    q4 = []
    for g in heads:
        parts = []
        for p in range(rep // 2):
            c0 = g * gw + p * KV_PACK
            pair = q_ref[:, c0:c0 + KV_PACK].astype(F32) * (dh ** -0.5 * LOG2E)
            parts.append(jnp.where(lane < dh, pair, 0.0))
            parts.append(jnp.where(lane < dh, pltpu.roll(pair, dh, 1), 0.0))
        q4.append(jnp.concatenate(parts, axis=0).astype(MXU_DTYPE))

    n_idx = lax.broadcasted_iota(jnp.int32, (1, n_cmp_rows), 1)
    valid_c = (n_idx * CMP_STRIDE + (CMP_BLOCK - 1) <= t_col) & (n_idx < n_cmp_rows - 1)
    bias_c = jnp.where(valid_c, 0.0, NEG_INF)
    o_c, p_sum = [], []
    for g in heads:
        kvc = cmp_ref[g]
        e_c, inv_c = _softmax2_parts(_dot_nt(q4[g], kvc).reshape(rep, qb, n_cmp_rows) + bias_c[None])
        p_c = e_c * inv_c
        o_c.append(_dot(p_c.reshape(rep * qb, n_cmp_rows), kvc))
        total = p_c[0]
        for r in range(1, rep):
            total = total + p_c[r]
        p_sum.append(total)

    jj = lax.broadcasted_iota(jnp.int32, (n_slc, qb), 0)
    cur = t_row >> sel_shift
    forced = (jj == 0) | (jj == cur) | (jj == cur - 1)
    causal = jj <= cur
    jf = jj.astype(F32)
    imp = []
    for g in heads:
        acc = None
        for term in _split3(p_sum[g]):
            d = _dot_nt(ovl_ref[...], term)
            acc = d if acc is None else acc + d
        imp.append(jnp.where(causal, jnp.where(forced, KNOCKOUT, acc), -FORCE))
    sel = [forced.astype(F32) for _ in heads]
    for _ in range(k_sel - N_FORCED):
        for g in heads:
            mx = jnp.max(imp[g], axis=0, keepdims=True)
            first = jnp.min(jnp.where(imp[g] == mx, jf, float(n_slc)), axis=0, keepdims=True)
            pick = jf == first
            sel[g] = jnp.where(pick, 1.0, sel[g])
            imp[g] = jnp.where(pick, KNOCKOUT, imp[g])
    before_own = jj < cur
    q_aug = []
    for g in heads:
        bias_t = jnp.where((sel[g] > 0.0) & before_own, 0.0, NEG_INF)
        neg_sel = bias_t.T.astype(MXU_DTYPE)
        q_aug.append(jnp.concatenate([q4[g], jnp.concatenate([neg_sel] * rep, axis=0)], axis=1))

    w0 = pl.multiple_of(jnp.maximum(t0 - WINDOW, 0), qb)
    kp = w0 + lax.broadcasted_iota(jnp.int32, (1, win_keys), 1)
    bias_w = jnp.where((kp <= t_col) & (kp > t_col - WINDOW), 0.0, NEG_INF)
    o_w = []
    for g in heads:
        kvw = kvw_ref[g, pl.ds(w0, win_keys), :]
        e_w, inv_w = _softmax2_parts(_dot_nt(q4[g], kvw).reshape(rep, qb, win_keys) + bias_w[None])
        o_w.append(_dot(e_w.reshape(rep * qb, win_keys), kvw) * inv_w.reshape(rep * qb, 1))

    m0, acc0 = [], []
    for g in heads:
        kv_own = kvs_ref[g, pl.ds(pl.multiple_of(t0, qb), qb), :]
        ov_own_t = jnp.where(lane < dh, 1.0, kv_own.astype(F32)).T
        s_d = _dot_nt(kv_own, q4[g]) + diag_ref[...]
        m0.append(jnp.max(s_d, axis=0, keepdims=True))
        acc0.append(_dot(ov_own_t, jnp.exp2(s_d - m0[g])))
    return [(q_aug[g], o_c[g], o_w[g], m0[g], acc0[g]) for g in heads]


def _nsa(q, gl, kv_cmp, kv6):
    b, s, _ = q.shape
    qt = min(NSA_Q_TILE, s)
    nb = s // qt
    gw = NSA_REP * NSA_HEAD_DIM
    n_cmp_rows = s // CMP_STRIDE
    n_slc = s // SEL_BLOCK
    kc = min(SEL_KEY_CHUNK, s)
    assert (s // kc) % 2 == 0, "the chunk loop walks key chunks in pairs"
    key_block = jnp.arange(s, dtype=jnp.int32)[:, None] // SEL_BLOCK
    blk_onehot = (key_block == jnp.arange(n_slc, dtype=jnp.int32)[None, :]).astype(MXU_DTYPE)
    sj = jnp.arange(n_slc, dtype=jnp.int32)[:, None] * SEL_BLOCK
    ci = jnp.arange(n_cmp_rows, dtype=jnp.int32)[None, :] * CMP_STRIDE
    overlap_t = ((ci < sj + SEL_BLOCK) & (ci + CMP_BLOCK > sj)).astype(MXU_DTYPE)
    ti = jnp.arange(qt, dtype=jnp.int32)
    own_block = (ti[:, None] // SEL_BLOCK == ti[None, :] // SEL_BLOCK) & (ti[:, None] <= ti[None, :])
    diag_bias = jnp.tile(jnp.where(own_block, 0.0, NEG_INF).astype(F32), (1, NSA_REP))
    ng = NSA_KV_HEADS
    const = lambda shape: pl.BlockSpec(shape, lambda bb, i: (0,) * len(shape), pipeline_mode=pl.Buffered(1))
    return pl.pallas_call(
        functools.partial(_nsa_kernel, seq=s),
        grid=(b, nb),
        in_specs=[
            pl.BlockSpec((None, qt, ng * gw), lambda bb, i: (bb, i, 0)),
            pl.BlockSpec((None, qt, GATE_PAD), lambda bb, i: (bb, i, 0)),
            pl.BlockSpec((None, ng, n_cmp_rows, KV_PACK), lambda bb, i: (bb, 0, 0, 0)),
            pl.BlockSpec((None, ng, s, KV_PACK), lambda bb, i: (bb, 1, 0, 0)),
            pl.BlockSpec((None, ng, s, KV_PACK), lambda bb, i: (bb, 2, 0, 0)),
            const((s, n_slc)), const((n_slc, n_cmp_rows)), const((qt, NSA_REP * qt)),
        ],
        out_specs=pl.BlockSpec((None, qt, ng * gw), lambda bb, i: (bb, i, 0)),
        out_shape=jax.ShapeDtypeStruct((b, s, NSA_WIDTH), MXU_DTYPE),
        scratch_shapes=[pltpu.VMEM((ng, 2, kc, NSA_REP * qt), F32),
                        pltpu.VMEM((ng, 2, kc, NSA_REP * qt), MXU_DTYPE),
                        pltpu.VMEM((ng, s // kc, KV_PACK, kc), MXU_DTYPE)],
        compiler_params=_params("parallel", "arbitrary"),
        name="nsa_attention",
    )(q, gl, kv_cmp, kv6, kv6, blk_onehot, overlap_t, diag_bias)


def _mix_out_kernel(x_ref, nsa_ref, zg_ref, zc_ref, zch_ref, gln_g_ref, gln_b_ref, ws_ref, bs_ref,
                    dww_ref, dwb_ref, cln_g_ref, cln_b_ref, og_ref, wo_ref, o_ref, hbuf_ref):
    tm = x_ref.shape[0]
    i = pl.program_id(1)

    def glu(z):
        z = z.astype(F32)
        return z[:, :CONV_WIDTH] * jax.nn.sigmoid(z[:, CONV_WIDTH:])

    halo = glu(zch_ref[...])
    hbuf_ref[0, 0:CONV_HALO, :] = jnp.where(i == 0, 0.0, halo)
    hbuf_ref[0, CONV_HALO:CONV_HALO + tm, :] = glu(zc_ref[...])
    n_shifted = CONV_HALO + tm - F32_SUBLANES
    for c in range(1, F32_SUBLANES):
        hbuf_ref[c, 0:n_shifted, :] = hbuf_ref[0, c:c + n_shifted, :]
    conv = jnp.broadcast_to(dwb_ref[...], (tm, CONV_WIDTH))
    for k in range(CONV_TAPS):
        off = CONV_HALO - (CONV_TAPS - 1) + k
        c = off % F32_SUBLANES
        conv = conv + dww_ref[k:k + 1, :] * hbuf_ref[c, off - c:off - c + tm, :]
    y_conv = jax.nn.silu(_layer_norm(conv, cln_g_ref[...], cln_b_ref[...]))

    zg = _gelu_tanh(zg_ref[...].astype(F32))
    u = zg[:, :GMLP_WIDTH]
    v = _layer_norm(zg[:, GMLP_WIDTH:], gln_g_ref[...], gln_b_ref[...]).astype(MXU_DTYPE)
    row = lax.broadcasted_iota(jnp.int32, (GMLP_CHUNK, GMLP_CHUNK), 0)
    col = lax.broadcasted_iota(jnp.int32, (GMLP_CHUNK, GMLP_CHUNK), 1)
    grp = lax.broadcasted_iota(jnp.int32, (GMLP_CHUNK, GMLP_WIDTH), 1) // (GMLP_WIDTH // GMLP_GROUPS)
    w_causal = [jnp.where(row >= col, ws_ref[g], 0.0).astype(MXU_DTYPE) for g in range(GMLP_GROUPS)]
    bias = jnp.zeros((GMLP_CHUNK, GMLP_WIDTH), F32)
    for g in range(GMLP_GROUPS):
        bias = jnp.where(grp == g, bs_ref[:, g:g + 1], bias)
    mixed = []
    for c in range(tm // GMLP_CHUNK):
        vc = v[c * GMLP_CHUNK:(c + 1) * GMLP_CHUNK]
        acc = bias
        for g in range(GMLP_GROUPS):
            acc = acc + jnp.where(grp == g, jnp.dot(w_causal[g], vc, preferred_element_type=F32), 0.0)
        mixed.append(acc)
    y_gmlp = u * jnp.concatenate(mixed, axis=0)

    og = og_ref[...]
    n0, n1 = NSA_WIDTH, NSA_WIDTH + GMLP_WIDTH
    y = _dot(_rms(nsa_ref[...].astype(F32), og[:, :n0]), wo_ref[:n0, :])
    y = y + _dot(_rms(y_gmlp, og[:, n0:n1]), wo_ref[n0:n1, :])
    y = y + _dot(_rms(y_conv, og[:, n1:]), wo_ref[n1:, :])
    o_ref[...] = x_ref[...] + y


def _mix_out(x, y_nsa, zg, zc, p, w_out, layer):
    b, s, d = x.shape
    tm = min(MIX_ROWS, s)
    halo_blocks = tm // CONV_HALO
    row2 = lambda a: a.reshape(1, -1)
    full = lambda shape: pl.BlockSpec(shape, lambda bb, i: (0,) * len(shape))
    tile = lambda w: pl.BlockSpec((None, tm, w), lambda bb, i: (bb, i, 0))
    return pl.pallas_call(
        _mix_out_kernel,
        grid=(b, s // tm),
        in_specs=[
            tile(d), tile(NSA_WIDTH), tile(2 * GMLP_WIDTH), tile(2 * CONV_WIDTH),
            pl.BlockSpec((None, CONV_HALO, 2 * CONV_WIDTH),
                         lambda bb, i: (bb, jnp.maximum(i * halo_blocks - 1, 0), 0)),
            full((1, GMLP_WIDTH)), full((1, GMLP_WIDTH)),
            full((GMLP_GROUPS, GMLP_CHUNK, GMLP_CHUNK)), full((GMLP_CHUNK, GMLP_GROUPS)),
            full((CONV_TAPS, CONV_WIDTH)), full((1, CONV_WIDTH)), full((1, CONV_WIDTH)), full((1, CONV_WIDTH)),
            full((1, d)), _layer_spec(w_out, layer),
        ],
        out_specs=tile(d),
        out_shape=jax.ShapeDtypeStruct((b, s, d), F32),
        scratch_shapes=[pltpu.VMEM((F32_SUBLANES, CONV_HALO + tm, CONV_WIDTH), F32)],
        compiler_params=_params("parallel", "parallel"),
        name="mix_out",
    )(x, y_nsa, zg, zc, zc, row2(p["gmlp_ln_g"]), row2(p["gmlp_ln_b"]), p["gmlp_ws"], p["gmlp_bs"].T,
      p["conv_dw_w"], row2(p["conv_dw_b"]), row2(p["conv_ln_g"]), row2(p["conv_ln_b"]),
      row2(p["mix_out_g"]), w_out)


def _mem_kv_kernel(m_ref, g_ref, wk_ref, wv_ref, k_ref, v_ref):
    m = _rms(m_ref[...], g_ref[...]).astype(MXU_DTYPE)
    k_ref[...] = _dot(m, wk_ref[...]).astype(k_ref.dtype)
    v_ref[...] = _dot(m, wv_ref[...]).astype(v_ref.dtype)


def _mem_kv(mem, g, wk, wv, layer):
    b, n, d = mem.shape
    blk = pl.BlockSpec((None, n, d), lambda bb: (bb, 0, 0))
    return pl.pallas_call(
        _mem_kv_kernel,
        grid=(b,),
        in_specs=[blk, pl.BlockSpec((1, d), lambda bb: (0, 0)), _layer_spec(wk, layer), _layer_spec(wv, layer)],
        out_specs=[blk, blk],
        out_shape=[jax.ShapeDtypeStruct((b, n, d), MXU_DTYPE)] * 2,
        compiler_params=_params("parallel"),
        name="xattn_mem_kv",
    )(mem, g.reshape(1, d), wk, wv)


def _xattn_kernel(x_ref, g_ref, wq_ref, k_ref, v_ref, wo_ref, o_ref, att_ref):
    dh = XATTN_HEAD_DIM
    heads = [slice(hd * dh, (hd + 1) * dh) for hd in range(XATTN_HEADS)]
    x = x_ref[...]
    h = _rms(x, g_ref[...]).astype(MXU_DTYPE)
    q = (_dot(h, wq_ref[...]) * (dh ** -0.5 * LOG2E)).astype(MXU_DTYPE)
    s = [_dot_nt(q[:, cols], k_ref[:, cols]) for cols in heads]
    e = [jnp.exp2(sh - jnp.max(sh, axis=-1, keepdims=True)) for sh in s]
    inv = [1.0 / jnp.sum(eh, axis=-1, keepdims=True) for eh in e]
    for cols, eh, ih in zip(heads, e, inv):
        att_ref[:, cols] = (_dot(eh, v_ref[:, cols]) * ih).astype(att_ref.dtype)
    o_ref[...] = x + _dot(att_ref[...], wo_ref[...])


def _xattn(x, g, wq, k, v, wo, layer):
    b, s, d = x.shape
    n = k.shape[1]
    tm = min(XATTN_ROWS, s)
    tile = pl.BlockSpec((None, tm, d), lambda bb, i: (bb, i, 0))
    mspec = pl.BlockSpec((None, n, d), lambda bb, i: (bb, 0, 0))
    return pl.pallas_call(
        _xattn_kernel,
        grid=(b, s // tm),
        in_specs=[tile, pl.BlockSpec((1, d), lambda bb, i: (0, 0)), _layer_spec(wq, layer), mspec, mspec,
                  _layer_spec(wo, layer)],
        out_specs=tile,
        out_shape=jax.ShapeDtypeStruct((b, s, d), F32),
        scratch_shapes=[pltpu.VMEM((tm, d), MXU_DTYPE)],
        compiler_params=_params("parallel", "parallel"),
        name="xattn",
    )(x, g.reshape(1, d), wq, k, v, wo)


def _ffn_kernel(x_ref, xh_ref, g_ref, wup_ref, cw_ref, cb_ref, wd_ref, fg_ref, o_ref, xn_ref, h_buf, act_buf,
                *, final_norm):
    tm = x_ref.shape[0]
    tc = FFN_COLS
    i = pl.program_id(1)
    halo = jnp.where(i == 0, 0.0, xh_ref[...])
    xn_ref[0:FFN_HALO, :] = _rms(halo, g_ref[...]).astype(xn_ref.dtype)
    xn_ref[FFN_HALO:FFN_HALO + tm, :] = _rms(x_ref[...], g_ref[...]).astype(xn_ref.dtype)
    xn = xn_ref[...]

    def conv(slot, half, c0):
        cols = slice(c0, c0 + tc)
        y = cb_ref[:, cols]
        for k in range(FFN_CONV_TAPS):
            shift = FFN_CONV_TAPS - 1 - k
            y = y + cw_ref[k:k + 1, cols] * h_buf[slot, half, FFN_HALO - shift:FFN_HALO - shift + tm, :]
        return y

    def project(j):
        gate_c0, up_c0 = j * tc, D_FF + j * tc
        h_buf[j % 2, 0] = _dot(xn, wup_ref[:, gate_c0:gate_c0 + tc])
        h_buf[j % 2, 1] = _dot(xn, wup_ref[:, up_c0:up_c0 + tc])

    n_blocks = D_FF // tc
    project(0)
    for j in range(n_blocks):
        if j + 1 < n_blocks:
            project(j + 1)
        act = jax.nn.silu(conv(j % 2, 0, j * tc)) * conv(j % 2, 1, D_FF + j * tc)
        act_buf[:, j * tc:(j + 1) * tc] = act.astype(act_buf.dtype)

    y = x_ref[...] + _dot(act_buf[...], wd_ref[...])
    o_ref[...] = _rms(y, fg_ref[...]) if final_norm else y


def _ffn(x, g, w_up, dw_w, dw_b, w_down, final_g, final_norm, layer):
    b, s, d = x.shape
    tm = min(FFN_ROWS, s)
    halo_blocks = tm // FFN_HALO
    tile = pl.BlockSpec((None, tm, d), lambda bb, i: (bb, i, 0))
    once = dict(pipeline_mode=pl.Buffered(1))
    resident = lambda shape: pl.BlockSpec(shape, lambda bb, i: (0,) * len(shape), **once)
    return pl.pallas_call(
        functools.partial(_ffn_kernel, final_norm=final_norm),
        grid=(b, s // tm),
        in_specs=[
            tile,
            pl.BlockSpec((None, FFN_HALO, d), lambda bb, i: (bb, jnp.maximum(i * halo_blocks - 1, 0), 0)),
            resident((1, d)),
            _layer_spec(w_up, layer, **once),
            resident((FFN_CONV_TAPS, 2 * D_FF)),
            resident((1, 2 * D_FF)),
            _layer_spec(w_down, layer, **once),
            resident((1, d)),
        ],
        out_specs=tile,
        out_shape=jax.ShapeDtypeStruct((b, s, d), F32),
        scratch_shapes=[pltpu.VMEM((FFN_HALO + tm, d), MXU_DTYPE),
                        pltpu.VMEM((2, 2, FFN_HALO + tm, FFN_COLS), F32),
                        pltpu.VMEM((tm, D_FF), MXU_DTYPE)],
        compiler_params=_params("parallel", "parallel"),
        name="conv_ffn",
    )(x, x, g.reshape(1, d), w_up, dw_w, dw_b.reshape(1, -1), w_down, final_g.reshape(1, d))


def kernel(x, mem, mix_norm_g, w_in, cmp_pe, cmp_w1, cmp_w2, gmlp_ln_g, gmlp_ln_b, gmlp_ws, gmlp_bs, conv_dw_w, conv_dw_b, conv_ln_g, conv_ln_b, mix_out_g, w_out, xattn_norm_g, mem_norm_g, xattn_wq, xattn_wk, xattn_wv, xattn_wo, ffn_norm_g, ffn_w_up, ffn_dw_w, ffn_dw_b, ffn_w_down, final_norm_g):
    depth = w_in.shape[0]
    cast = lambda w: w.astype(MXU_DTYPE)
    w_in_p = _permute_w_in(w_in)
    w_out_c, wq_c, wk_c, wv_c, wo_c = cast(w_out), cast(xattn_wq), cast(xattn_wk), cast(xattn_wv), cast(xattn_wo)
    w_up_c, w_down_c = cast(ffn_w_up), cast(ffn_w_down)
    for l in range(depth):
        q, kv6, zg, zc, gl = _in_proj(x, mix_norm_g[l], w_in_p, l)
        kv_cmp = _compress(kv6, *_pack_cmp_weights(cmp_pe[l], cmp_w1[l], cmp_w2[l]))
        y_nsa = _nsa(q, gl, kv_cmp, kv6)
        mix = dict(gmlp_ln_g=gmlp_ln_g[l], gmlp_ln_b=gmlp_ln_b[l], gmlp_ws=gmlp_ws[l], gmlp_bs=gmlp_bs[l],
                   conv_dw_w=conv_dw_w[l], conv_dw_b=conv_dw_b[l], conv_ln_g=conv_ln_g[l],
                   conv_ln_b=conv_ln_b[l], mix_out_g=mix_out_g[l])
        x = _mix_out(x, y_nsa, zg, zc, mix, w_out_c, l)
        mk, mv = _mem_kv(mem, mem_norm_g[l], wk_c, wv_c, l)
        x = _xattn(x, xattn_norm_g[l], wq_c, mk, mv, wo_c, l)
        x = _ffn(x, ffn_norm_g[l], w_up_c, ffn_dw_w[l], ffn_dw_b[l], w_down_c, final_norm_g,
                 final_norm=(l == depth - 1), layer=l)
    return x
```

```python
import functools

import jax
import jax.numpy as jnp
from jax import lax
from jax.experimental import pallas as pl
from jax.experimental.pallas import tpu as pltpu

D_MODEL = 1024
NSA_HEADS = 8
NSA_KV_HEADS = 2
NSA_HEAD_DIM = 64
NSA_REP = NSA_HEADS // NSA_KV_HEADS
CMP_BLOCK = 32
CMP_STRIDE = 16
SEL_BLOCK = 64
N_SELECT = 16
WINDOW = 512
NSA_WIDTH = NSA_HEADS * NSA_HEAD_DIM
KV_WIDTH = NSA_KV_HEADS * NSA_HEAD_DIM
GMLP_WIDTH = 256
GMLP_GROUPS = 4
GMLP_CHUNK = 128
CONV_WIDTH = 256
CONV_TAPS = 31
XATTN_HEADS = 4
XATTN_HEAD_DIM = D_MODEL // XATTN_HEADS
D_FF = 2816
FFN_CONV_TAPS = 3
EPS = 1e-6
NEG_INF = -1e30
TINY = 1e-30
FORCE = 1e4
LOG2E = 1.4426950408889634
MAX_FLOOR = -1e29
KNOCKOUT = -3.0e38
N_FORCED = 3

LANES = 128
F32_SUBLANES = 8
BF16_SUBLANES = 16
VMEM_LIMIT_BYTES = 48 * 1024 * 1024

MXU_DTYPE = jnp.bfloat16
F32 = jnp.float32

IN_PROJ_ROWS = 1024
NSA_Q_TILE = 256
SEL_KEY_CHUNK = 512
MIX_ROWS = 512
CONV_HALO = 32
XATTN_ROWS = 1024
FFN_ROWS = 512
FFN_COLS = 256
FFN_HALO = BF16_SUBLANES
GATE_PAD = LANES
KV_PACK = 2 * NSA_HEAD_DIM


def _params(*semantics):
    return pltpu.CompilerParams(dimension_semantics=semantics, vmem_limit_bytes=VMEM_LIMIT_BYTES)


def _dot(a, b):
    return jnp.dot(a.astype(MXU_DTYPE), b.astype(MXU_DTYPE), preferred_element_type=F32)


def _dot_nt(a, b):
    return lax.dot_general(a.astype(MXU_DTYPE), b.astype(MXU_DTYPE), (((1,), (1,)), ((), ())),
                           preferred_element_type=F32)


def _rms(x, g):
    return x * lax.rsqrt(jnp.mean(x * x, axis=-1, keepdims=True) + EPS) * g


def _layer_norm(x, g, b):
    mu = jnp.mean(x, axis=-1, keepdims=True)
    xc = x - mu
    var = jnp.mean(xc * xc, axis=-1, keepdims=True)
    return xc * lax.rsqrt(var + EPS) * g + b


def _gelu_tanh(x):
    return 0.5 * x * (1.0 + jnp.tanh(0.7978845608028654 * (x + 0.044715 * (x * x * x))))


IN_Q = (0, NSA_WIDTH)
IN_KV = (IN_Q[1], IN_Q[1] + 6 * KV_PACK)
IN_ZG = (IN_KV[1], IN_KV[1] + 2 * GMLP_WIDTH)
IN_ZC = (IN_ZG[1], IN_ZG[1] + 2 * CONV_WIDTH)
IN_GL = (IN_ZC[1], IN_ZC[1] + GATE_PAD)


def _layer_spec(stacked, layer, **kwargs):
    shape = stacked.shape[1:]
    return pl.BlockSpec((None,) + shape, lambda *_: (layer,) + (0,) * len(shape), **kwargs)


def _permute_w_in(w):
    off_kv = NSA_WIDTH
    off_gl = NSA_WIDTH + 6 * KV_WIDTH
    off_zg = off_gl + NSA_HEADS * 3
    off_zc = off_zg + 2 * GMLP_WIDTH
    parts = [w[..., :NSA_WIDTH]]
    for branch in range(3):
        k0 = off_kv + (2 * branch) * KV_WIDTH
        v0 = off_kv + (2 * branch + 1) * KV_WIDTH
        for g in range(NSA_KV_HEADS):
            parts.append(w[..., k0 + g * NSA_HEAD_DIM:k0 + (g + 1) * NSA_HEAD_DIM])
            parts.append(w[..., v0 + g * NSA_HEAD_DIM:v0 + (g + 1) * NSA_HEAD_DIM])
    parts.append(w[..., off_zg:off_zg + 2 * GMLP_WIDTH])
    parts.append(w[..., off_zc:off_zc + 2 * CONV_WIDTH])
    n_gate = NSA_HEADS * 3
    parts.append(jnp.pad(w[..., off_gl:off_gl + n_gate], ((0, 0), (0, 0), (0, GATE_PAD - n_gate))))
    return jnp.concatenate(parts, axis=-1).astype(MXU_DTYPE)


def _in_proj_kernel(x_ref, g_ref, w_ref, q_ref, kv_ref, zg_ref, zc_ref, gl_ref):
    h = _rms(x_ref[...], g_ref[...]).astype(MXU_DTYPE)
    q_ref[...] = _dot(h, w_ref[:, IN_Q[0]:IN_Q[1]]).astype(q_ref.dtype)
    kv = _dot(h, w_ref[:, IN_KV[0]:IN_KV[1]])
    for j in range(6):
        kv_ref[j] = kv[:, j * KV_PACK:(j + 1) * KV_PACK].astype(kv_ref.dtype)
    zg_ref[...] = _dot(h, w_ref[:, IN_ZG[0]:IN_ZG[1]]).astype(zg_ref.dtype)
    zc_ref[...] = _dot(h, w_ref[:, IN_ZC[0]:IN_ZC[1]]).astype(zc_ref.dtype)
    gl_ref[...] = _dot(h, w_ref[:, IN_GL[0]:IN_GL[1]])


def _in_proj(x, g, w, layer):
    b, s, d = x.shape
    tm = IN_PROJ_ROWS
    return pl.pallas_call(
        _in_proj_kernel,
        grid=(b, s // tm),
        in_specs=[
            pl.BlockSpec((None, tm, d), lambda bb, i: (bb, i, 0)),
            pl.BlockSpec((1, d), lambda bb, i: (0, 0)),
            _layer_spec(w, layer),
        ],
        out_specs=[
            pl.BlockSpec((None, tm, NSA_WIDTH), lambda bb, i: (bb, i, 0)),
            pl.BlockSpec((None, 6, tm, KV_PACK), lambda bb, i: (bb, 0, i, 0)),
            pl.BlockSpec((None, tm, 2 * GMLP_WIDTH), lambda bb, i: (bb, i, 0)),
            pl.BlockSpec((None, tm, 2 * CONV_WIDTH), lambda bb, i: (bb, i, 0)),
            pl.BlockSpec((None, tm, GATE_PAD), lambda bb, i: (bb, i, 0)),
        ],
        out_shape=[
            jax.ShapeDtypeStruct((b, s, NSA_WIDTH), MXU_DTYPE),
            jax.ShapeDtypeStruct((b, 6, s, KV_PACK), MXU_DTYPE),
            jax.ShapeDtypeStruct((b, s, 2 * GMLP_WIDTH), MXU_DTYPE),
            jax.ShapeDtypeStruct((b, s, 2 * CONV_WIDTH), MXU_DTYPE),
            jax.ShapeDtypeStruct((b, s, GATE_PAD), F32),
        ],
        compiler_params=_params("parallel", "parallel"),
        name="in_proj",
    )(x, g.reshape(1, d), w)


CMP_ROW = CMP_STRIDE * KV_PACK


def _pack_cmp_weights(pe, w1, w2):
    dh = NSA_HEAD_DIM
    w1r = w1.reshape(2, CMP_BLOCK, dh, dh)
    z = jnp.zeros_like(w1r[0])
    wk = jnp.concatenate([w1r[0], z], axis=-1)
    wv = jnp.concatenate([z, w1r[1]], axis=-1)
    w1p = jnp.concatenate([wk, wv], axis=1).reshape(CMP_BLOCK * KV_PACK, KV_PACK)
    z2 = jnp.zeros_like(w2[0])
    w2p = jnp.concatenate([jnp.concatenate([w2[0], z2], axis=1), jnp.concatenate([z2, w2[1]], axis=1)], axis=0)
    pep = jnp.concatenate([pe[0], pe[1]], axis=-1).reshape(1, CMP_BLOCK * KV_PACK)
    pep = jnp.broadcast_to(pep, (F32_SUBLANES, CMP_BLOCK * KV_PACK))
    return pep.astype(MXU_DTYPE), w1p.astype(MXU_DTYPE), w2p.astype(MXU_DTYPE)


def _compress_kernel(c_ref, pe_ref, w1_ref, w2_ref, o_ref):
    c = c_ref[...]
    n = c.shape[0]
    first = _dot(c, w1_ref[:CMP_ROW, :])
    second = _dot(c, w1_ref[CMP_ROW:, :])
    pe_term = _dot(pe_ref[...], w1_ref[...])[0:1, :]
    h = first + pltpu.roll(second, n - 1, 0) + pe_term
    o_ref[...] = _dot(_gelu_tanh(h), w2_ref[...]).astype(o_ref.dtype)


def _compress(kv6, pep, w1p, w2p):
    b, _, s, _ = kv6.shape
    n = s // CMP_STRIDE
    c = kv6[:, :NSA_KV_HEADS].reshape(b, NSA_KV_HEADS, n, CMP_ROW)
    return pl.pallas_call(
        _compress_kernel,
        grid=(b, NSA_KV_HEADS),
        in_specs=[
            pl.BlockSpec((None, None, n, CMP_ROW), lambda bb, g: (bb, g, 0, 0)),
            pl.BlockSpec((F32_SUBLANES, 2 * CMP_ROW), lambda bb, g: (0, 0)),
            pl.BlockSpec((2 * CMP_ROW, KV_PACK), lambda bb, g: (0, 0)),
            pl.BlockSpec((KV_PACK, KV_PACK), lambda bb, g: (0, 0)),
        ],
        out_specs=pl.BlockSpec((None, None, n, KV_PACK), lambda bb, g: (bb, g, 0, 0)),
        out_shape=jax.ShapeDtypeStruct((b, NSA_KV_HEADS, n, KV_PACK), MXU_DTYPE),
        compiler_params=_params("parallel", "parallel"),
        name="nsa_compress",
    )(c, pep, w1p, w2p)


def _split3(x):
    a = x.astype(MXU_DTYPE)
    r = x - a.astype(F32)
    b = r.astype(MXU_DTYPE)
    c = (r - b.astype(F32)).astype(MXU_DTYPE)
    return a, b, c


def _softmax2_parts(s):
    m = jnp.maximum(jnp.max(s, axis=-1, keepdims=True), MAX_FLOOR)
    e = jnp.exp2(s - m)
    return e, 1.0 / jnp.maximum(jnp.sum(e, axis=-1, keepdims=True), TINY)


def _nsa_kernel(q_ref, gl_ref, cmp_ref, kvs_ref, kvw_ref, blk_ref, ovl_ref, diag_ref, o_ref, s_buf, e_buf, ovt_ref,
                *, seq):
    dh, qb, rep, groups = NSA_HEAD_DIM, q_ref.shape[0], NSA_REP, NSA_KV_HEADS
    n_cmp_rows = seq // CMP_STRIDE
    n_slc = seq // SEL_BLOCK
    k_sel = min(N_SELECT, n_slc)
    kc = min(SEL_KEY_CHUNK, seq)
    win_keys = WINDOW + qb
    sel_shift = SEL_BLOCK.bit_length() - 1
    gw = rep * dh

    bi = pl.program_id(1)
    t0 = bi * qb

    @pl.when(bi == 0)
    def _():
        chunk_lanes = lax.broadcasted_iota(jnp.int32, (kc, KV_PACK), 1)
        for g in range(groups):
            for c in range(seq // kc):
                ov = jnp.where(chunk_lanes < dh, 1.0, kvs_ref[g, c * kc:(c + 1) * kc, :].astype(F32))
                ovt_ref[g, c] = ov.T.astype(ovt_ref.dtype)

    t_col = t0 + lax.broadcasted_iota(jnp.int32, (qb, 1), 0)
    t_row = t0 + lax.broadcasted_iota(jnp.int32, (1, qb), 1)
    lane = lax.broadcasted_iota(jnp.int32, (qb, KV_PACK), 1)

    prepared = _nsa_prepare(q_ref, cmp_ref, kvs_ref, kvw_ref, ovl_ref, diag_ref, t0, t_col, t_row, lane,
                            seq=seq, qb=qb)
    q_aug = [p[0] for p in prepared]

    def scores(g, c, slot):
        k0 = pl.multiple_of(c * kc, kc)
        keys = jnp.concatenate([kvs_ref[g, pl.ds(k0, kc), :], blk_ref[pl.ds(k0, kc), :]], axis=1)
        s_buf[g, slot] = _dot_nt(keys, q_aug[g])

    def softmax_step(g, slot, m):
        s = s_buf[g, slot]
        m_new = jnp.maximum(m, jnp.max(s, axis=0, keepdims=True))
        e_buf[g, slot] = jnp.exp2(s - m_new).astype(e_buf.dtype)
        return m_new, jnp.exp2(m - m_new)

    def weighted_values(g, c, slot, alpha, acc):
        return alpha * acc + _dot(ovt_ref[g, c], e_buf[g, slot])

    last_chunk = seq // kc - 1

    def pair_body(i, carry):
        c = 2 * i
        heads = range(groups)
        m, acc, alpha_prev = ([carry[g][k] for g in heads] for k in range(3))
        alpha_even, alpha_odd = [None] * groups, [None] * groups
        for g in heads:
            scores(g, c + 1, 1)
        for g in heads:
            m[g], alpha_even[g] = softmax_step(g, 0, m[g])
        for g in heads:
            acc[g] = weighted_values(g, jnp.maximum(c - 1, 0), 1, alpha_prev[g], acc[g])
        for g in heads:
            scores(g, jnp.minimum(c + 2, last_chunk), 0)
        for g in heads:
            m[g], alpha_odd[g] = softmax_step(g, 1, m[g])
        for g in heads:
            acc[g] = weighted_values(g, c, 0, alpha_even[g], acc[g])
        return tuple((m[g], acc[g], alpha_odd[g]) for g in heads)

    n_pairs = ((t0 + qb - SEL_BLOCK + kc - 1) // kc + 1) // 2
    for g in range(groups):
        scores(g, 0, 0)
        e_buf[g, 1] = jnp.zeros(e_buf.shape[2:], e_buf.dtype)
    init = tuple((p[3], p[4], jnp.ones_like(p[3])) for p in prepared)
    final = lax.fori_loop(0, n_pairs, pair_body, init)

    lane4 = lax.broadcasted_iota(jnp.int32, (rep * qb, KV_PACK), 1)
    gates = jax.nn.sigmoid(gl_ref[...])
    for g in range(groups):
        _, o_c, o_w, _, _ = prepared[g]
        _, acc_s, alpha_last = final[g]
        acc_s = weighted_values(g, 2 * n_pairs - 1, 1, alpha_last, acc_s).T
        o_s = acc_s / jnp.where(lane4 < dh, 1.0, pltpu.roll(acc_s, dh, 1))

        outs = []
        for r in range(rep):
            rows = slice(r * qb, (r + 1) * qb)
            gc = 3 * (g * rep + r)
            outs.append(gates[:, gc:gc + 1] * o_c[rows] + gates[:, gc + 1:gc + 2] * o_s[rows]
                        + gates[:, gc + 2:gc + 3] * o_w[rows])
        for p in range(rep // 2):
            pair = jnp.where(lane < dh, pltpu.roll(outs[2 * p], dh, 1), outs[2 * p + 1])
            c0 = g * gw + p * KV_PACK
            o_ref[:, c0:c0 + KV_PACK] = pair.astype(o_ref.dtype)


def _nsa_prepare(q_ref, cmp_ref, kvs_ref, kvw_ref, ovl_ref, diag_ref, t0, t_col, t_row, lane, *, seq, qb):
    dh, rep = NSA_HEAD_DIM, NSA_REP
    heads = range(NSA_KV_HEADS)
    n_cmp_rows = seq // CMP_STRIDE
    n_slc = seq // SEL_BLOCK
    k_sel = min(N_SELECT, n_slc)
    win_keys = WINDOW + qb
    sel_shift = SEL_BLOCK.bit_length() - 1
    gw = rep * dh

    q4 = []
    for g in heads:
        parts = []
        for p in range(rep // 2):
            c0 = g * gw + p * KV_PACK
            pair = q_ref[:, c0:c0 + KV_PACK].astype(F32) * (dh ** -0.5 * LOG2E)
            parts.append(jnp.where(lane < dh, pair, 0.0))
            parts.append(jnp.where(lane < dh, pltpu.roll(pair, dh, 1), 0.0))
        q4.append(jnp.concatenate(parts, axis=0).astype(MXU_DTYPE))

    n_idx = lax.broadcasted_iota(jnp.int32, (1, n_cmp_rows), 1)
    valid_c = (n_idx * CMP_STRIDE + (CMP_BLOCK - 1) <= t_col) & (n_idx < n_cmp_rows - 1)
    bias_c = jnp.where(valid_c, 0.0, NEG_INF)
    o_c, p_sum = [], []
    for g in heads:
        kvc = cmp_ref[g]
        e_c, inv_c = _softmax2_parts(_dot_nt(q4[g], kvc).reshape(rep, qb, n_cmp_rows) + bias_c[None])
        p_c = e_c * inv_c
        o_c.append(_dot(p_c.reshape(rep * qb, n_cmp_rows), kvc))
        total = p_c[0]
        for r in range(1, rep):
            total = total + p_c[r]
        p_sum.append(total)

    jj = lax.broadcasted_iota(jnp.int32, (n_slc, qb), 0)
    cur = t_row >> sel_shift
    forced = (jj == 0) | (jj == cur) | (jj == cur - 1)
    causal = jj <= cur
    jf = jj.astype(F32)
    imp = []
    for g in heads:
        acc = None
        for term in _split3(p_sum[g]):
            d = _dot_nt(ovl_ref[...], term)
            acc = d if acc is None else acc + d
        imp.append(jnp.where(causal, jnp.where(forced, KNOCKOUT, acc), -FORCE))
    sel = [forced.astype(F32) for _ in heads]
    for _ in range(k_sel - N_FORCED):
        for g in heads:
            mx = jnp.max(imp[g], axis=0, keepdims=True)
            first = jnp.min(jnp.where(imp[g] == mx, jf, float(n_slc)), axis=0, keepdims=True)
            pick = jf == first
            sel[g] = jnp.where(pick, 1.0, sel[g])
            imp[g] = jnp.where(pick, KNOCKOUT, imp[g])
    before_own = jj < cur
    q_aug = []
    for g in heads:
        bias_t = jnp.where((sel[g] > 0.0) & before_own, 0.0, NEG_INF)
        neg_sel = bias_t.T.astype(MXU_DTYPE)
        q_aug.append(jnp.concatenate([q4[g], jnp.concatenate([neg_sel] * rep, axis=0)], axis=1))

    w0 = pl.multiple_of(jnp.maximum(t0 - WINDOW, 0), qb)
    kp = w0 + lax.broadcasted_iota(jnp.int32, (1, win_keys), 1)
    bias_w = jnp.where((kp <= t_col) & (kp > t_col - WINDOW), 0.0, NEG_INF)
    o_w = []
    for g in heads:
        kvw = kvw_ref[g, pl.ds(w0, win_keys), :]
        e_w, inv_w = _softmax2_parts(_dot_nt(q4[g], kvw).reshape(rep, qb, win_keys) + bias_w[None])
        o_w.append(_dot(e_w.reshape(rep * qb, win_keys), kvw) * inv_w.reshape(rep * qb, 1))

    m0, acc0 = [], []
    for g in heads:
        kv_own = kvs_ref[g, pl.ds(pl.multiple_of(t0, qb), qb), :]
        ov_own_t = jnp.where(lane < dh, 1.0, kv_own.astype(F32)).T
        s_d = _dot_nt(kv_own, q4[g]) + diag_ref[...]
        m0.append(jnp.max(s_d, axis=0, keepdims=True))
        acc0.append(_dot(ov_own_t, jnp.exp2(s_d - m0[g])))
    return [(q_aug[g], o_c[g], o_w[g], m0[g], acc0[g]) for g in heads]


def _nsa(q, gl, kv_cmp, kv6):
    b, s, _ = q.shape
    qt = min(NSA_Q_TILE, s)
    nb = s // qt
    gw = NSA_REP * NSA_HEAD_DIM
    n_cmp_rows = s // CMP_STRIDE
    n_slc = s // SEL_BLOCK
    kc = min(SEL_KEY_CHUNK, s)
    assert (s // kc) % 2 == 0, "the chunk loop walks key chunks in pairs"
    key_block = jnp.arange(s, dtype=jnp.int32)[:, None] // SEL_BLOCK
    blk_onehot = (key_block == jnp.arange(n_slc, dtype=jnp.int32)[None, :]).astype(MXU_DTYPE)
    sj = jnp.arange(n_slc, dtype=jnp.int32)[:, None] * SEL_BLOCK
    ci = jnp.arange(n_cmp_rows, dtype=jnp.int32)[None, :] * CMP_STRIDE
    overlap_t = ((ci < sj + SEL_BLOCK) & (ci + CMP_BLOCK > sj)).astype(MXU_DTYPE)
    ti = jnp.arange(qt, dtype=jnp.int32)
    own_block = (ti[:, None] // SEL_BLOCK == ti[None, :] // SEL_BLOCK) & (ti[:, None] <= ti[None, :])
    diag_bias = jnp.tile(jnp.where(own_block, 0.0, NEG_INF).astype(F32), (1, NSA_REP))
    ng = NSA_KV_HEADS
    const = lambda shape: pl.BlockSpec(shape, lambda bb, i: (0,) * len(shape), pipeline_mode=pl.Buffered(1))
    return pl.pallas_call(
        functools.partial(_nsa_kernel, seq=s),
        grid=(b, nb),
        in_specs=[
            pl.BlockSpec((None, qt, ng * gw), lambda bb, i: (bb, i, 0)),
            pl.BlockSpec((None, qt, GATE_PAD), lambda bb, i: (bb, i, 0)),
            pl.BlockSpec((None, ng, n_cmp_rows, KV_PACK), lambda bb, i: (bb, 0, 0, 0)),
            pl.BlockSpec((None, ng, s, KV_PACK), lambda bb, i: (bb, 1, 0, 0)),
            pl.BlockSpec((None, ng, s, KV_PACK), lambda bb, i: (bb, 2, 0, 0)),
            const((s, n_slc)), const((n_slc, n_cmp_rows)), const((qt, NSA_REP * qt)),
        ],
        out_specs=pl.BlockSpec((None, qt, ng * gw), lambda bb, i: (bb, i, 0)),
        out_shape=jax.ShapeDtypeStruct((b, s, NSA_WIDTH), MXU_DTYPE),
        scratch_shapes=[pltpu.VMEM((ng, 2, kc, NSA_REP * qt), F32),
                        pltpu.VMEM((ng, 2, kc, NSA_REP * qt), MXU_DTYPE),
                        pltpu.VMEM((ng, s // kc, KV_PACK, kc), MXU_DTYPE)],
        compiler_params=_params("parallel", "arbitrary"),
        name="nsa_attention",
    )(q, gl, kv_cmp, kv6, kv6, blk_onehot, overlap_t, diag_bias)


def _mix_out_kernel(x_ref, nsa_ref, zg_ref, zc_ref, zch_ref, gln_g_ref, gln_b_ref, ws_ref, bs_ref,
                    dww_ref, dwb_ref, cln_g_ref, cln_b_ref, og_ref, wo_ref, o_ref, hbuf_ref):
    tm = x_ref.shape[0]
    i = pl.program_id(1)

    def glu(z):
        z = z.astype(F32)
        return z[:, :CONV_WIDTH] * jax.nn.sigmoid(z[:, CONV_WIDTH:])

    halo = glu(zch_ref[...])
    hbuf_ref[0, 0:CONV_HALO, :] = jnp.where(i == 0, 0.0, halo)
    hbuf_ref[0, CONV_HALO:CONV_HALO + tm, :] = glu(zc_ref[...])
    n_shifted = CONV_HALO + tm - F32_SUBLANES
    for c in range(1, F32_SUBLANES):
        hbuf_ref[c, 0:n_shifted, :] = hbuf_ref[0, c:c + n_shifted, :]
    conv = jnp.broadcast_to(dwb_ref[...], (tm, CONV_WIDTH))
    for k in range(CONV_TAPS):
        off = CONV_HALO - (CONV_TAPS - 1) + k
        c = off % F32_SUBLANES
        conv = conv + dww_ref[k:k + 1, :] * hbuf_ref[c, off - c:off - c + tm, :]
    y_conv = jax.nn.silu(_layer_norm(conv, cln_g_ref[...], cln_b_ref[...]))

    zg = _gelu_tanh(zg_ref[...].astype(F32))
    u = zg[:, :GMLP_WIDTH]
    v = _layer_norm(zg[:, GMLP_WIDTH:], gln_g_ref[...], gln_b_ref[...]).astype(MXU_DTYPE)
    row = lax.broadcasted_iota(jnp.int32, (GMLP_CHUNK, GMLP_CHUNK), 0)
    col = lax.broadcasted_iota(jnp.int32, (GMLP_CHUNK, GMLP_CHUNK), 1)
    grp = lax.broadcasted_iota(jnp.int32, (GMLP_CHUNK, GMLP_WIDTH), 1) // (GMLP_WIDTH // GMLP_GROUPS)
    w_causal = [jnp.where(row >= col, ws_ref[g], 0.0).astype(MXU_DTYPE) for g in range(GMLP_GROUPS)]
    bias = jnp.zeros((GMLP_CHUNK, GMLP_WIDTH), F32)
    for g in range(GMLP_GROUPS):
        bias = jnp.where(grp == g, bs_ref[:, g:g + 1], bias)
    mixed = []
    for c in range(tm // GMLP_CHUNK):
        vc = v[c * GMLP_CHUNK:(c + 1) * GMLP_CHUNK]
        acc = bias
        for g in range(GMLP_GROUPS):
            acc = acc + jnp.where(grp == g, jnp.dot(w_causal[g], vc, preferred_element_type=F32), 0.0)
        mixed.append(acc)
    y_gmlp = u * jnp.concatenate(mixed, axis=0)

    og = og_ref[...]
    n0, n1 = NSA_WIDTH, NSA_WIDTH + GMLP_WIDTH
    y = _dot(_rms(nsa_ref[...].astype(F32), og[:, :n0]), wo_ref[:n0, :])
    y = y + _dot(_rms(y_gmlp, og[:, n0:n1]), wo_ref[n0:n1, :])
    y = y + _dot(_rms(y_conv, og[:, n1:]), wo_ref[n1:, :])
    o_ref[...] = x_ref[...] + y


def _mix_out(x, y_nsa, zg, zc, p, w_out, layer):
    b, s, d = x.shape
    tm = min(MIX_ROWS, s)
    halo_blocks = tm // CONV_HALO
    row2 = lambda a: a.reshape(1, -1)
    full = lambda shape: pl.BlockSpec(shape, lambda bb, i: (0,) * len(shape))
    tile = lambda w: pl.BlockSpec((None, tm, w), lambda bb, i: (bb, i, 0))
    return pl.pallas_call(
        _mix_out_kernel,
        grid=(b, s // tm),
        in_specs=[
            tile(d), tile(NSA_WIDTH), tile(2 * GMLP_WIDTH), tile(2 * CONV_WIDTH),
            pl.BlockSpec((None, CONV_HALO, 2 * CONV_WIDTH),
                         lambda bb, i: (bb, jnp.maximum(i * halo_blocks - 1, 0), 0)),
            full((1, GMLP_WIDTH)), full((1, GMLP_WIDTH)),
            full((GMLP_GROUPS, GMLP_CHUNK, GMLP_CHUNK)), full((GMLP_CHUNK, GMLP_GROUPS)),
            full((CONV_TAPS, CONV_WIDTH)), full((1, CONV_WIDTH)), full((1, CONV_WIDTH)), full((1, CONV_WIDTH)),
            full((1, d)), _layer_spec(w_out, layer),
        ],
        out_specs=tile(d),
        out_shape=jax.ShapeDtypeStruct((b, s, d), F32),
        scratch_shapes=[pltpu.VMEM((F32_SUBLANES, CONV_HALO + tm, CONV_WIDTH), F32)],
        compiler_params=_params("parallel", "parallel"),
        name="mix_out",
    )(x, y_nsa, zg, zc, zc, row2(p["gmlp_ln_g"]), row2(p["gmlp_ln_b"]), p["gmlp_ws"], p["gmlp_bs"].T,
      p["conv_dw_w"], row2(p["conv_dw_b"]), row2(p["conv_ln_g"]), row2(p["conv_ln_b"]),
      row2(p["mix_out_g"]), w_out)


def _mem_kv_kernel(m_ref, g_ref, wk_ref, wv_ref, k_ref, v_ref):
    m = _rms(m_ref[...], g_ref[...]).astype(MXU_DTYPE)
    k_ref[...] = _dot(m, wk_ref[...]).astype(k_ref.dtype)
    v_ref[...] = _dot(m, wv_ref[...]).astype(v_ref.dtype)


def _mem_kv(mem, g, wk, wv, layer):
    b, n, d = mem.shape
    blk = pl.BlockSpec((None, n, d), lambda bb: (bb, 0, 0))
    return pl.pallas_call(
        _mem_kv_kernel,
        grid=(b,),
        in_specs=[blk, pl.BlockSpec((1, d), lambda bb: (0, 0)), _layer_spec(wk, layer), _layer_spec(wv, layer)],
        out_specs=[blk, blk],
        out_shape=[jax.ShapeDtypeStruct((b, n, d), MXU_DTYPE)] * 2,
        compiler_params=_params("parallel"),
        name="xattn_mem_kv",
    )(mem, g.reshape(1, d), wk, wv)


def _xattn_kernel(x_ref, g_ref, wq_ref, k_ref, v_ref, wo_ref, o_ref, att_ref):
    dh = XATTN_HEAD_DIM
    heads = [slice(hd * dh, (hd + 1) * dh) for hd in range(XATTN_HEADS)]
    x = x_ref[...]
    h = _rms(x, g_ref[...]).astype(MXU_DTYPE)
    q = (_dot(h, wq_ref[...]) * (dh ** -0.5 * LOG2E)).astype(MXU_DTYPE)
    s = [_dot_nt(q[:, cols], k_ref[:, cols]) for cols in heads]
    e = [jnp.exp2(sh - jnp.max(sh, axis=-1, keepdims=True)) for sh in s]
    inv = [1.0 / jnp.sum(eh, axis=-1, keepdims=True) for eh in e]
    for cols, eh, ih in zip(heads, e, inv):
        att_ref[:, cols] = (_dot(eh, v_ref[:, cols]) * ih).astype(att_ref.dtype)
    o_ref[...] = x + _dot(att_ref[...], wo_ref[...])


def _xattn(x, g, wq, k, v, wo, layer):
    b, s, d = x.shape
    n = k.shape[1]
    tm = min(XATTN_ROWS, s)
    tile = pl.BlockSpec((None, tm, d), lambda bb, i: (bb, i, 0))
    mspec = pl.BlockSpec((None, n, d), lambda bb, i: (bb, 0, 0))
    return pl.pallas_call(
        _xattn_kernel,
        grid=(b, s // tm),
        in_specs=[tile, pl.BlockSpec((1, d), lambda bb, i: (0, 0)), _layer_spec(wq, layer), mspec, mspec,
                  _layer_spec(wo, layer)],
        out_specs=tile,
        out_shape=jax.ShapeDtypeStruct((b, s, d), F32),
        scratch_shapes=[pltpu.VMEM((tm, d), MXU_DTYPE)],
        compiler_params=_params("parallel", "parallel"),
        name="xattn",
    )(x, g.reshape(1, d), wq, k, v, wo)


def _ffn_kernel(x_ref, xh_ref, g_ref, wup_ref, cw_ref, cb_ref, wd_ref, fg_ref, o_ref, xn_ref, h_buf, act_buf,
                *, final_norm):
    tm = x_ref.shape[0]
    tc = FFN_COLS
    i = pl.program_id(1)
    halo = jnp.where(i == 0, 0.0, xh_ref[...])
    xn_ref[0:FFN_HALO, :] = _rms(halo, g_ref[...]).astype(xn_ref.dtype)
    xn_ref[FFN_HALO:FFN_HALO + tm, :] = _rms(x_ref[...], g_ref[...]).astype(xn_ref.dtype)
    xn = xn_ref[...]

    def conv(slot, half, c0):
        cols = slice(c0, c0 + tc)
        y = cb_ref[:, cols]
        for k in range(FFN_CONV_TAPS):
            shift = FFN_CONV_TAPS - 1 - k
            y = y + cw_ref[k:k + 1, cols] * h_buf[slot, half, FFN_HALO - shift:FFN_HALO - shift + tm, :]
        return y

    def project(j):
        gate_c0, up_c0 = j * tc, D_FF + j * tc
        h_buf[j % 2, 0] = _dot(xn, wup_ref[:, gate_c0:gate_c0 + tc])
        h_buf[j % 2, 1] = _dot(xn, wup_ref[:, up_c0:up_c0 + tc])

    n_blocks = D_FF // tc
    project(0)
    for j in range(n_blocks):
        if j + 1 < n_blocks:
            project(j + 1)
        act = jax.nn.silu(conv(j % 2, 0, j * tc)) * conv(j % 2, 1, D_FF + j * tc)
        act_buf[:, j * tc:(j + 1) * tc] = act.astype(act_buf.dtype)

    y = x_ref[...] + _dot(act_buf[...], wd_ref[...])
    o_ref[...] = _rms(y, fg_ref[...]) if final_norm else y


def _ffn(x, g, w_up, dw_w, dw_b, w_down, final_g, final_norm, layer):
    b, s, d = x.shape
    tm = min(FFN_ROWS, s)
    halo_blocks = tm // FFN_HALO
    tile = pl.BlockSpec((None, tm, d), lambda bb, i: (bb, i, 0))
    once = dict(pipeline_mode=pl.Buffered(1))
    resident = lambda shape: pl.BlockSpec(shape, lambda bb, i: (0,) * len(shape), **once)
    return pl.pallas_call(
        functools.partial(_ffn_kernel, final_norm=final_norm),
        grid=(b, s // tm),
        in_specs=[
            tile,
            pl.BlockSpec((None, FFN_HALO, d), lambda bb, i: (bb, jnp.maximum(i * halo_blocks - 1, 0), 0)),
            resident((1, d)),
            _layer_spec(w_up, layer, **once),
            resident((FFN_CONV_TAPS, 2 * D_FF)),
            resident((1, 2 * D_FF)),
            _layer_spec(w_down, layer, **once),
            resident((1, d)),
        ],
        out_specs=tile,
        out_shape=jax.ShapeDtypeStruct((b, s, d), F32),
        scratch_shapes=[pltpu.VMEM((FFN_HALO + tm, d), MXU_DTYPE),
                        pltpu.VMEM((2, 2, FFN_HALO + tm, FFN_COLS), F32),
                        pltpu.VMEM((tm, D_FF), MXU_DTYPE)],
        compiler_params=_params("parallel", "parallel"),
        name="conv_ffn",
    )(x, x, g.reshape(1, d), w_up, dw_w, dw_b.reshape(1, -1), w_down, final_g.reshape(1, d))


def kernel(x, mem, mix_norm_g, w_in, cmp_pe, cmp_w1, cmp_w2, gmlp_ln_g, gmlp_ln_b, gmlp_ws, gmlp_bs, conv_dw_w, conv_dw_b, conv_ln_g, conv_ln_b, mix_out_g, w_out, xattn_norm_g, mem_norm_g, xattn_wq, xattn_wk, xattn_wv, xattn_wo, ffn_norm_g, ffn_w_up, ffn_dw_w, ffn_dw_b, ffn_w_down, final_norm_g):
    depth = w_in.shape[0]
    cast = lambda w: w.astype(MXU_DTYPE)
    w_in_p = _permute_w_in(w_in)
    w_out_c, wq_c, wk_c, wv_c, wo_c = cast(w_out), cast(xattn_wq), cast(xattn_wk), cast(xattn_wv), cast(xattn_wo)
    w_up_c, w_down_c = cast(ffn_w_up), cast(ffn_w_down)
    for l in range(depth):
        q, kv6, zg, zc, gl = _in_proj(x, mix_norm_g[l], w_in_p, l)
        kv_cmp = _compress(kv6, *_pack_cmp_weights(cmp_pe[l], cmp_w1[l], cmp_w2[l]))
        y_nsa = _nsa(q, gl, kv_cmp, kv6)
        mix = dict(gmlp_ln_g=gmlp_ln_g[l], gmlp_ln_b=gmlp_ln_b[l], gmlp_ws=gmlp_ws[l], gmlp_bs=gmlp_bs[l],
                   conv_dw_w=conv_dw_w[l], conv_dw_b=conv_dw_b[l], conv_ln_g=conv_ln_g[l],
                   conv_ln_b=conv_ln_b[l], mix_out_g=mix_out_g[l])
        x = _mix_out(x, y_nsa, zg, zc, mix, w_out_c, l)
        mk, mv = _mem_kv(mem, mem_norm_g[l], wk_c, wv_c, l)
        x = _xattn(x, xattn_norm_g[l], wq_c, mk, mv, wo_c, l)
        x = _ffn(x, ffn_norm_g[l], w_up_c, ffn_dw_w[l], ffn_dw_b[l], w_down_c, final_norm_g,
                 final_norm=(l == depth - 1), layer=l)
    return x
```
